```python
import jax, jax.numpy as jnp
from jax import lax
import numpy as np

D_MODEL = 2048
BATCH = 2
SEQ = 16384
DEPTH = 1

POOL_WIDTH = D_MODEL // 2
POOL_GROUPS = 4
POOL_WINDOWS = (2, 4, 8, 16)
POOL_GROUP_DIM = POOL_WIDTH // POOL_GROUPS
HEAD_DIM = 128
N_HEADS = (D_MODEL // 2) // HEAD_DIM
ATTN_WIDTH = N_HEADS * HEAD_DIM
Q_BLOCK = 128
N_BRANCHES = 2
OFF_Q = POOL_WIDTH
OFF_K = OFF_Q + ATTN_WIDTH
OFF_V = OFF_K + ATTN_WIDTH
OFF_F = OFF_V + ATTN_WIDTH
OFF_GATE = OFF_F + N_HEADS
IN_WIDTH = OFF_GATE + N_BRANCHES * D_MODEL
N_EXPERTS = 64
TOP_K = 6
EXPERT_DIM = 1408 * D_MODEL // 2048
SHARED_DIM = 2 * EXPERT_DIM
ROUTE_SCALE = 1.0
EXPERT_BLOCK = 256
ALPHA = (2 * DEPTH) ** 0.25
BETA = (8 * DEPTH) ** -0.25
LN_EPS = 1e-5

kernel_name = "pool_fox_gated_hybrid_moe_deepnorm"


def layer_norm(x, g, b):
    x32 = x.astype(jnp.float32)
    mu = jnp.mean(x32, axis=-1, keepdims=True)
    var = jnp.mean(jnp.square(x32 - mu), axis=-1, keepdims=True)
    y = (x32 - mu) * lax.rsqrt(var + LN_EPS) * g.astype(jnp.float32) + b.astype(jnp.float32)
    return y.astype(x.dtype)


def pool_mixer(u, pool_w, pool_scale):
    B, S, _ = u.shape
    u32 = u.astype(jnp.float32)
    c = jnp.cumsum(u32, axis=1)
    c = jnp.concatenate([jnp.zeros_like(c[:, :1]), c], axis=1)
    t1 = jnp.arange(1, S + 1)
    outs = []
    for g, w in enumerate(POOL_WINDOWS):
        sl = slice(g * POOL_GROUP_DIM, (g + 1) * POOL_GROUP_DIM)
        cg = c[:, :, sl]
        lo = jnp.maximum(t1 - w, 0)
        cnt = jnp.minimum(t1, w).astype(jnp.float32)
        mean = (cg[:, 1:] - cg[:, lo]) / cnt[None, :, None]
        outs.append(mean - u32[:, :, sl])
    p = jnp.stack(outs, axis=2).astype(u.dtype)
    y = jnp.einsum('bsgc,gcd->bsgd', p, pool_w).reshape(B, S, POOL_WIDTH)
    return y * pool_scale


def forgetting_attention(q, k, v, log_f):
    B, S, H, Dh = q.shape
    F = jnp.cumsum(log_f, axis=1).transpose(0, 2, 1)
    qh = q.transpose(0, 2, 1, 3) * (Dh ** -0.5)
    kh = k.transpose(0, 2, 1, 3)
    vh = v.transpose(0, 2, 1, 3)
    k_pos = jnp.arange(S)
    n_blk = S // Q_BLOCK

    def block(i):
        start = i * Q_BLOCK
        qb = lax.dynamic_slice_in_dim(qh, start, Q_BLOCK, axis=2)
        fq = lax.dynamic_slice_in_dim(F, start, Q_BLOCK, axis=2)
        s = jnp.einsum('bhqd,bhkd->bhqk', qb, kh).astype(jnp.float32)
        s = s + fq[..., :, None] - F[..., None, :]
        q_pos = start + jnp.arange(Q_BLOCK)
        s = jnp.where(k_pos[None, :] <= q_pos[:, None], s, -jnp.inf)
        p = jax.nn.softmax(s, axis=-1).astype(vh.dtype)
        return jnp.einsum('bhqk,bhkd->bhqd', p, vh)

    o = lax.map(block, jnp.arange(n_blk))
    return o.transpose(1, 0, 3, 2, 4).reshape(B, S, H * Dh)


def hybrid_mixer(x, w_in, b_forget, pool_w, pool_scale, w_branch_pool, w_branch_attn, w_out):
    B, S, _ = x.shape
    z = jnp.einsum('bsd,de->bse', x, w_in)
    u_pool = z[..., :OFF_Q]
    q = z[..., OFF_Q:OFF_K].reshape(B, S, N_HEADS, HEAD_DIM)
    k = z[..., OFF_K:OFF_V].reshape(B, S, N_HEADS, HEAD_DIM)
    v = z[..., OFF_V:OFF_F].reshape(B, S, N_HEADS, HEAD_DIM)
    log_f = jax.nn.log_sigmoid(z[..., OFF_F:OFF_GATE].astype(jnp.float32) + b_forget.astype(jnp.float32))
    gates = jax.nn.sigmoid(z[..., OFF_GATE:].astype(jnp.float32)).astype(x.dtype)
    gates = gates.reshape(B, S, N_BRANCHES, D_MODEL)
    y_pool = jnp.einsum('bsc,cd->bsd', pool_mixer(u_pool, pool_w, pool_scale), w_branch_pool)
    y_attn = jnp.einsum('bsc,cd->bsd', forgetting_attention(q, k, v, log_f), w_branch_attn)
    merged = gates[:, :, 0] * y_pool + gates[:, :, 1] * y_attn
    return jnp.einsum('bsd,de->bse', merged, w_out)


def swiglu(h, w1, w3, w2):
    return (jax.nn.silu(h @ w1) * (h @ w3)) @ w2


def routed_experts(h, w_router, router_bias, w1, w3, w2):
    N, D = h.shape
    scores = jax.nn.sigmoid(jnp.dot(h, w_router).astype(jnp.float32))
    _, idx = lax.top_k(scores + router_bias.astype(jnp.float32), TOP_K)
    gate = jnp.take_along_axis(scores, idx, axis=1)
    gate = gate / jnp.sum(gate, axis=1, keepdims=True) * ROUTE_SCALE
    A = N * TOP_K
    e_flat = idx.reshape(A)
    tok_flat = jnp.repeat(jnp.arange(N, dtype=jnp.int32), TOP_K)
    g_flat = gate.reshape(A)
    order = jnp.argsort(e_flat)
    e_s, tok_s, g_s = e_flat[order], tok_flat[order], g_flat[order]
    counts = jnp.zeros((N_EXPERTS,), jnp.int32).at[e_flat].add(1)
    padded = (counts + EXPERT_BLOCK - 1) // EXPERT_BLOCK * EXPERT_BLOCK
    start = jnp.cumsum(counts) - counts
    pend = jnp.cumsum(padded)
    pstart = pend - padded
    dest = pstart[e_s] + (jnp.arange(A, dtype=jnp.int32) - start[e_s])
    n_blocks = -(-A // EXPERT_BLOCK) + N_EXPERTS
    R = n_blocks * EXPERT_BLOCK
    buf_tok = jnp.zeros((R,), jnp.int32).at[dest].set(tok_s).reshape(n_blocks, EXPERT_BLOCK)
    buf_gate = jnp.zeros((R,), jnp.float32).at[dest].set(g_s).reshape(n_blocks, EXPERT_BLOCK)
    blk_expert = jnp.searchsorted(pend, jnp.arange(n_blocks, dtype=jnp.int32) * EXPERT_BLOCK, side='right')
    blk_expert = jnp.minimum(blk_expert, N_EXPERTS - 1)

    def body(acc, xs):
        e, tok, g = xs
        y = swiglu(h[tok], w1[e], w3[e], w2[e])
        return acc.at[tok].add(y.astype(jnp.float32) * g[:, None]), None

    acc, _ = lax.scan(body, jnp.zeros((N, D), jnp.float32), (blk_expert, buf_tok, buf_gate))
    return acc.astype(h.dtype)


def setup_inputs(seed: int = 0) -> dict:
    key = jax.random.key(seed)
    ks = jax.random.split(key, 20)
    L, D, E, F, FS = DEPTH, D_MODEL, N_EXPERTS, EXPERT_DIM, SHARED_DIM
    nrm = jax.random.normal
    x = nrm(ks[0], (BATCH, SEQ, D), jnp.float32)
    w_in = nrm(ks[1], (L, D, IN_WIDTH), jnp.float32) * D ** -0.5
    w_in = w_in.at[:, :, OFF_V:OFF_F].multiply(BETA)
    b_forget = jax.random.uniform(ks[2], (L, N_HEADS), jnp.float32, 1.0, 5.0)
    pool_w = nrm(ks[3], (L, POOL_GROUPS, POOL_GROUP_DIM, POOL_GROUP_DIM), jnp.float32) * POOL_GROUP_DIM ** -0.5
    pool_scale = 1.0 + 0.1 * nrm(ks[4], (L, POOL_WIDTH), jnp.float32)
    w_branch_pool = nrm(ks[5], (L, POOL_WIDTH, D), jnp.float32) * POOL_WIDTH ** -0.5 * BETA
    w_branch_attn = nrm(ks[6], (L, ATTN_WIDTH, D), jnp.float32) * ATTN_WIDTH ** -0.5 * BETA
    w_out = nrm(ks[7], (L, D, D), jnp.float32) * D ** -0.5 * BETA
    ln1_g = 1.0 + 0.05 * nrm(ks[8], (L, D), jnp.float32)
    ln1_b = 0.02 * nrm(ks[9], (L, D), jnp.float32)
    w_router = nrm(ks[10], (L, D, E), jnp.float32) * D ** -0.5
    router_bias = 0.01 * nrm(ks[11], (L, E), jnp.float32)
    w1 = nrm(ks[12], (L, E, D, F), jnp.float32) * D ** -0.5 * BETA
    w3 = nrm(ks[13], (L, E, D, F), jnp.float32) * D ** -0.5 * BETA
    w2 = nrm(ks[14], (L, E, F, D), jnp.float32) * F ** -0.5 * BETA
    w_shared1 = nrm(ks[15], (L, D, FS), jnp.float32) * D ** -0.5 * BETA
    w_shared3 = nrm(ks[16], (L, D, FS), jnp.float32) * D ** -0.5 * BETA
    w_shared2 = nrm(ks[17], (L, FS, D), jnp.float32) * FS ** -0.5 * BETA
    ln2_g = 1.0 + 0.05 * nrm(ks[18], (L, D), jnp.float32)
    ln2_b = 0.02 * nrm(ks[19], (L, D), jnp.float32)
    return {"x": x, "w_in": w_in, "b_forget": b_forget, "pool_w": pool_w, "pool_scale": pool_scale,
            "w_branch_pool": w_branch_pool, "w_branch_attn": w_branch_attn, "w_out": w_out,
            "ln1_g": ln1_g, "ln1_b": ln1_b, "w_router": w_router, "router_bias": router_bias,
            "w1": w1, "w3": w3, "w2": w2, "w_shared1": w_shared1, "w_shared3": w_shared3,
            "w_shared2": w_shared2, "ln2_g": ln2_g, "ln2_b": ln2_b}


def reference(x, w_in, b_forget, pool_w, pool_scale, w_branch_pool, w_branch_attn, w_out,
              ln1_g, ln1_b, w_router, router_bias, w1, w3, w2, w_shared1, w_shared3,
              w_shared2, ln2_g, ln2_b):
    B, S, D = x.shape
    for l in range(DEPTH):
        mix = hybrid_mixer(x, w_in[l], b_forget[l], pool_w[l], pool_scale[l],
                           w_branch_pool[l], w_branch_attn[l], w_out[l])
        x = layer_norm(ALPHA * x + mix, ln1_g[l], ln1_b[l])
        h = x.reshape(B * S, D)
        ffn = swiglu(h, w_shared1[l], w_shared3[l], w_shared2[l]) + \
            routed_experts(h, w_router[l], router_bias[l], w1[l], w3[l], w2[l])
        x = layer_norm(ALPHA * x + ffn.reshape(B, S, D), ln2_g[l], ln2_b[l])
    return x
```

```python
import functools

import jax
import jax.numpy as jnp
from jax import lax
from jax.experimental import pallas as pl
from jax.experimental.pallas import tpu as pltpu

F32 = jnp.float32
BF16 = jnp.bfloat16
I32 = jnp.int32

POOL_WINDOWS = (2, 4, 8, 16)
HEAD_DIM = 128
TOP_K = 6
ROUTE_SCALE = 1.0
DEPTH = 1
ALPHA = (2 * DEPTH) ** 0.25
LN_EPS = 1e-5

LANES = 128
MAX_WINDOW = max(POOL_WINDOWS)
ATTN_BLOCK = 512
ROW_CHUNK = 1024
SUB_ROWS = 256
UP_K_TILE = 256
DOWN_N_TILE = 512
SHARED_F_TILE = 256
TOK_TILE = 256
VMEM_LIMIT = 56 * 1024 * 1024


def _cparams(sem, vmem=VMEM_LIMIT):
    return pltpu.CompilerParams(dimension_semantics=sem, vmem_limit_bytes=vmem)


def _split3(v):
    hi = v.astype(BF16)
    r1 = v - hi.astype(F32)
    mid = r1.astype(BF16)
    lo = (r1 - mid.astype(F32)).astype(BF16)
    return hi, mid, lo


def _layer_norm(v, g, b):
    mu = jnp.mean(v, axis=-1, keepdims=True)
    d = v - mu
    var = jnp.mean(d * d, axis=-1, keepdims=True)
    return d * lax.rsqrt(var + LN_EPS) * g + b


def _inproj_kernel(x_ref, w_ref, o_ref, xb_ref, *, bn, q_lo, q_hi, q_scale):
    j = pl.program_id(1)

    @pl.when(j == 0)
    def _():
        xb_ref[...] = x_ref[...].astype(BF16)

    acc = jnp.dot(xb_ref[...], w_ref[...], preferred_element_type=F32)
    col0 = j * bn
    s = jnp.where((col0 >= q_lo) & (col0 < q_hi), q_scale, 1.0).astype(F32)
    o_ref[...] = (acc * s).astype(BF16)


def _inproj(x2, w_main, *, q_lo, q_hi, q_scale, bm=1024, bn=512):
    n, d = x2.shape
    c = w_main.shape[1]
    return pl.pallas_call(
        functools.partial(_inproj_kernel, bn=bn, q_lo=q_lo, q_hi=q_hi, q_scale=q_scale),
        grid=(n // bm, c // bn),
        in_specs=[pl.BlockSpec((bm, d), lambda i, j: (i, 0)),
                  pl.BlockSpec((d, bn), lambda i, j: (0, j))],
        out_specs=pl.BlockSpec((bm, bn), lambda i, j: (i, j)),
        out_shape=jax.ShapeDtypeStruct((n, c), BF16),
        scratch_shapes=[pltpu.VMEM((bm, d), BF16)],
        compiler_params=_cparams(("parallel", "arbitrary")),
        name="inproj",
    )(x2, w_main)


def _forget_kernel(x_ref, wf_ref, b_ref, sel_ref, ka_ref, carry_ref, *, ts, n_heads):
    i = pl.program_id(1)

    @pl.when(i == 0)
    def _():
        carry_ref[...] = jnp.zeros_like(carry_ref)

    z = jnp.dot(x_ref[...].astype(BF16), wf_ref[...], preferred_element_type=F32) + b_ref[...]
    lf = jnp.minimum(z, 0.0) - jnp.log1p(jnp.exp(-jnp.abs(z)))
    lane = lax.broadcasted_iota(I32, lf.shape, 1)
    lf = jnp.where(lane < n_heads, lf, 0.0)

    row = lax.broadcasted_iota(I32, (ts, ts), 0)
    col = lax.broadcasted_iota(I32, (ts, ts), 1)
    tri = jnp.where(col <= row, 1.0, 0.0).astype(BF16)
    hi, mid, lo = _split3(lf)
    cs = (jnp.dot(tri, hi, preferred_element_type=F32)
          + jnp.dot(tri, mid, preferred_element_type=F32)
          + jnp.dot(tri, lo, preferred_element_type=F32)) + carry_ref[...]
    carry_ref[...] = cs[ts - 1:ts, :]

    h2, m2, l2 = _split3(-cs)
    ka = (jnp.dot(h2, sel_ref[0], preferred_element_type=F32)
          + jnp.dot(m2, sel_ref[1], preferred_element_type=F32)
          + jnp.dot(l2, sel_ref[2], preferred_element_type=F32))
    ka_ref[...] = ka.astype(BF16)


def _forget_bias(x2, wf_pad, b_pad, sel, *, batch, seq, n_heads, ts=512):
    n, d = x2.shape
    nt = seq // ts
    return pl.pallas_call(
        functools.partial(_forget_kernel, ts=ts, n_heads=n_heads),
        grid=(batch, nt),
        in_specs=[pl.BlockSpec((ts, d), lambda b, i: (b * nt + i, 0)),
                  pl.BlockSpec((d, LANES), lambda b, i: (0, 0)),
                  pl.BlockSpec((1, LANES), lambda b, i: (0, 0)),
                  pl.BlockSpec((3, LANES, n_heads * LANES), lambda b, i: (0, 0, 0))],
        out_specs=pl.BlockSpec((ts, n_heads * LANES), lambda b, i: (b * nt + i, 0)),
        out_shape=jax.ShapeDtypeStruct((n, n_heads * LANES), BF16),
        scratch_shapes=[pltpu.VMEM((1, LANES), F32)],
        compiler_params=_cparams(("parallel", "arbitrary")),
        name="forget_bias",
    )(x2, wf_pad, b_pad, sel)


def _pool_kernel(u_ref, halo_ref, pw_ref, sc_ref, o_ref, buf_ref, *, ts, gdim):
    i = pl.program_id(1)
    h = MAX_WINDOW
    halo = halo_ref[...].astype(F32)
    buf_ref[0:h, :] = jnp.where(i > 0, halo, 0.0)
    buf_ref[h:h + ts, :] = u_ref[...].astype(F32)

    t = i * ts + lax.broadcasted_iota(I32, (ts, 1), 0)
    for g, w in enumerate(POOL_WINDOWS):
        cs = slice(g * gdim, (g + 1) * gdim)
        tok = buf_ref[h:h + ts, cs]
        wsum = tok
        for k in range(1, w):
            wsum = wsum + buf_ref[h - k:h - k + ts, cs]
        cnt = jnp.minimum(t + 1, w).astype(F32)
        p = (wsum / cnt - tok).astype(BF16)
        y = jnp.dot(p, pw_ref[g], preferred_element_type=F32) * sc_ref[:, cs]
        o_ref[:, cs] = y.astype(BF16)


def _pool(z, pool_w_b, pool_scale2, *, batch, seq, width, ts=512):
    n = z.shape[0]
    groups = len(POOL_WINDOWS)
    gdim = width // groups
    nt = seq // ts
    hb = ts // MAX_WINDOW
    return pl.pallas_call(
        functools.partial(_pool_kernel, ts=ts, gdim=gdim),
        grid=(batch, nt),
        in_specs=[pl.BlockSpec((ts, width), lambda b, i: (b * nt + i, 0)),
                  pl.BlockSpec((MAX_WINDOW, width),
                               lambda b, i: (jnp.maximum((b * nt + i) * hb - 1, 0), 0)),
                  pl.BlockSpec((groups, gdim, gdim), lambda b, i: (0, 0, 0)),
                  pl.BlockSpec((1, width), lambda b, i: (0, 0))],
        out_specs=pl.BlockSpec((ts, width), lambda b, i: (b * nt + i, 0)),
        out_shape=jax.ShapeDtypeStruct((n, width), BF16),
        scratch_shapes=[pltpu.VMEM((ts + MAX_WINDOW, width), F32)],
        compiler_params=_cparams(("parallel", "parallel")),
        name="pool_mixer",
    )(z, z, pool_w_b, pool_scale2)


def _attn_kernel(q_ref, k_ref, ka_ref, v_ref, o_ref, m_ref, l_ref, acc_ref, *, blk):
    qi = pl.program_id(2)
    lane = lax.broadcasted_iota(I32, (blk, HEAD_DIM), 1)
    ones3 = jnp.where(lane < 3, 1.0, 0.0).astype(BF16)
    q_aug = jnp.concatenate([q_ref[...], ones3], axis=1)

    m_ref[...] = jnp.full_like(m_ref, -jnp.inf)
    l_ref[...] = jnp.zeros_like(l_ref)
    acc_ref[...] = jnp.zeros_like(acc_ref)
    reps = blk // LANES

    def step(ki, masked):
        ks = pl.multiple_of(ki * blk, blk)
        k_aug = jnp.concatenate([k_ref[pl.ds(ks, blk), :], ka_ref[pl.ds(ks, blk), :]], axis=1)
        s = lax.dot_general(q_aug, k_aug, (((1,), (1,)), ((), ())),
                            preferred_element_type=F32)
        if masked:
            r = lax.broadcasted_iota(I32, (blk, blk), 0)
            c = lax.broadcasted_iota(I32, (blk, blk), 1)
            s = jnp.where(c <= r, s, -jnp.inf)
        m_prev = m_ref[...]
        m_next = jnp.maximum(m_prev, jnp.max(s, axis=1, keepdims=True))
        alpha = jnp.exp(m_prev - m_next)
        p = jnp.exp(s - jnp.concatenate([m_next] * reps, axis=1))
        l_ref[...] = alpha * l_ref[...] + jnp.sum(p, axis=1, keepdims=True)
        acc_ref[...] = alpha * acc_ref[...] + jnp.dot(
            p.astype(BF16), v_ref[pl.ds(ks, blk), :], preferred_element_type=F32)
        m_ref[...] = m_next

    def body(ki, carry):
        step(ki, False)
        return carry

    lax.fori_loop(0, qi, body, 0)
    step(qi, True)
    o_ref[...] = (acc_ref[...] / l_ref[...]).astype(BF16)


def _attention(z, ka, *, batch, seq, n_heads, q_col, k_col, v_col, blk=ATTN_BLOCK):
    n = z.shape[0]
    nq = seq // blk
    return pl.pallas_call(
        functools.partial(_attn_kernel, blk=blk),
        grid=(batch, n_heads, nq),
        in_specs=[pl.BlockSpec((blk, HEAD_DIM), lambda b, h, i: (b * nq + i, q_col + h)),
                  pl.BlockSpec((seq, HEAD_DIM), lambda b, h, i: (b, k_col + h)),
                  pl.BlockSpec((seq, HEAD_DIM), lambda b, h, i: (b, h)),
                  pl.BlockSpec((seq, HEAD_DIM), lambda b, h, i: (b, v_col + h))],
        out_specs=pl.BlockSpec((blk, HEAD_DIM), lambda b, h, i: (b * nq + i, h)),
        out_shape=jax.ShapeDtypeStruct((n, n_heads * HEAD_DIM), BF16),
        scratch_shapes=[pltpu.VMEM((blk, LANES), F32),
                        pltpu.VMEM((blk, LANES), F32),
                        pltpu.VMEM((blk, HEAD_DIM), F32)],
        compiler_params=_cparams(("parallel", "parallel", "arbitrary")),
        name="fox_attention",
    )(z, z, ka, z)


def _merge_kernel(x_ref, p_ref, a_ref, wg0_ref, wg1_ref, wbp_ref, wba_ref, o_ref, xb_ref):
    j = pl.program_id(1)

    @pl.when(j == 0)
    def _():
        xb_ref[...] = x_ref[...].astype(BF16)

    xb = xb_ref[...]
    g0 = jax.nn.sigmoid(jnp.dot(xb, wg0_ref[...], preferred_element_type=F32))
    g1 = jax.nn.sigmoid(jnp.dot(xb, wg1_ref[...], preferred_element_type=F32))
    yp = jnp.dot(p_ref[...], wbp_ref[...], preferred_element_type=F32)
    ya = jnp.dot(a_ref[...], wba_ref[...], preferred_element_type=F32)
    o_ref[...] = (g0 * yp + g1 * ya).astype(BF16)


def _merge(x2, pool_o, attn_o, wg, wbp, wba, *, bm=512, bn=512):
    n, d = x2.shape
    wp = pool_o.shape[1]
    wa = attn_o.shape[1]
    nj = d // bn
    return pl.pallas_call(
        _merge_kernel,
        grid=(n // bm, nj),
        in_specs=[pl.BlockSpec((bm, d), lambda i, j: (i, 0)),
                  pl.BlockSpec((bm, wp), lambda i, j: (i, 0)),
                  pl.BlockSpec((bm, wa), lambda i, j: (i, 0)),
                  pl.BlockSpec((d, bn), lambda i, j: (0, j)),
                  pl.BlockSpec((d, bn), lambda i, j: (0, nj + j)),
                  pl.BlockSpec((wp, bn), lambda i, j: (0, j)),
                  pl.BlockSpec((wa, bn), lambda i, j: (0, j))],
        out_specs=pl.BlockSpec((bm, bn), lambda i, j: (i, j)),
        out_shape=jax.ShapeDtypeStruct((n, d), BF16),
        scratch_shapes=[pltpu.VMEM((bm, d), BF16)],
        compiler_params=_cparams(("parallel", "arbitrary")),
        name="branch_merge",
    )(x2, pool_o, attn_o, wg, wg, wbp, wba)


def _out_ln_router_kernel(x_ref, m_ref, wo_ref, g_ref, b_ref, wrh_ref, wrl_ref,
                          x1_ref, x1b_ref, lg_ref):
    mix = jnp.dot(m_ref[...], wo_ref[...], preferred_element_type=F32)
    x1 = _layer_norm(ALPHA * x_ref[...] + mix, g_ref[...], b_ref[...])
    x1_ref[...] = x1
    hi = x1.astype(BF16)
    x1b_ref[...] = hi
    lo = (x1 - hi.astype(F32)).astype(BF16)
    lg_ref[...] = (jnp.dot(hi, wrh_ref[...], preferred_element_type=F32)
                   + jnp.dot(lo, wrh_ref[...], preferred_element_type=F32)
                   + jnp.dot(hi, wrl_ref[...], preferred_element_type=F32))


def _out_ln_router(x2, merged, wo, g1, b1, wr_hi, wr_lo, *, bm=512):
    n, d = x2.shape
    row = lambda i: (i, 0)
    fix = lambda i: (0, 0)
    return pl.pallas_call(
        _out_ln_router_kernel,
        grid=(n // bm,),
        in_specs=[pl.BlockSpec((bm, d), row), pl.BlockSpec((bm, d), row),
                  pl.BlockSpec((d, d), fix), pl.BlockSpec((1, d), fix), pl.BlockSpec((1, d), fix),
                  pl.BlockSpec((d, LANES), fix), pl.BlockSpec((d, LANES), fix)],
        out_specs=[pl.BlockSpec((bm, d), row), pl.BlockSpec((bm, d), row),
                   pl.BlockSpec((bm, LANES), row)],
        out_shape=[jax.ShapeDtypeStruct((n, d), F32), jax.ShapeDtypeStruct((n, d), BF16),
                   jax.ShapeDtypeStruct((n, LANES), F32)],
        compiler_params=_cparams(("parallel",)),
        name="outproj_ln1_router",
    )(x2, merged, wo, g1, b1, wr_hi, wr_lo)


def _route_kernel(lg_ref, bias_ref, idx_ref, gate_ref, rank_ref, cnt_ref, carry_ref,
                  *, bm, n_experts):
    i = pl.program_id(0)

    @pl.when(i == 0)
    def _():
        carry_ref[...] = jnp.zeros_like(carry_ref)

    lane = lax.broadcasted_iota(I32, (bm, LANES), 1)
    lane_f = lane.astype(F32)
    scores = jax.nn.sigmoid(lg_ref[...])
    sel = jnp.where(lane < n_experts, scores + bias_ref[...], -jnp.inf)

    hits, gates = [], []
    gsum = jnp.zeros((bm, 1), F32)
    member = jnp.zeros((bm, LANES), F32)
    idx_out = jnp.zeros((bm, LANES), F32)
    for k in range(TOP_K):
        m = jnp.max(sel, axis=1, keepdims=True)
        ik = jnp.min(jnp.where(sel == m, lane_f, float(LANES)), axis=1, keepdims=True)
        hit = lane_f == ik
        gk = jnp.sum(jnp.where(hit, scores, 0.0), axis=1, keepdims=True)
        sel = jnp.where(hit, -jnp.inf, sel)
        member = jnp.where(hit, 1.0, member)
        idx_out = jnp.where(lane == k, ik, idx_out)
        gsum = gsum + gk
        hits.append(hit)
        gates.append(gk)

    row = lax.broadcasted_iota(I32, (bm, bm), 0)
    col = lax.broadcasted_iota(I32, (bm, bm), 1)
    before = jnp.where(col < row, 1.0, 0.0).astype(BF16)
    prefix = jnp.dot(before, member.astype(BF16), preferred_element_type=F32) + carry_ref[...]

    gate_out = jnp.zeros((bm, LANES), F32)
    rank_out = jnp.zeros((bm, LANES), F32)
    for k in range(TOP_K):
        rk = jnp.sum(jnp.where(hits[k], prefix, 0.0), axis=1, keepdims=True)
        rank_out = jnp.where(lane == k, rk, rank_out)
        gate_out = jnp.where(lane == k, gates[k] / gsum * ROUTE_SCALE, gate_out)

    total = carry_ref[...] + jnp.sum(member, axis=0, keepdims=True)
    carry_ref[...] = total
    idx_ref[...] = idx_out.astype(I32)
    gate_ref[...] = gate_out
    rank_ref[...] = rank_out.astype(I32)
    cnt_ref[...] = jnp.broadcast_to(total, cnt_ref.shape).astype(I32)


def _route(logits, bias_pad, *, n_experts, bm=TOK_TILE):
    n = logits.shape[0]
    row = lambda i: (i, 0)
    fix = lambda i: (0, 0)
    return pl.pallas_call(
        functools.partial(_route_kernel, bm=bm, n_experts=n_experts),
        grid=(n // bm,),
        in_specs=[pl.BlockSpec((bm, LANES), row), pl.BlockSpec((1, LANES), fix)],
        out_specs=[pl.BlockSpec((bm, LANES), row), pl.BlockSpec((bm, LANES), row),
                   pl.BlockSpec((bm, LANES), row), pl.BlockSpec((8, LANES), fix)],
        out_shape=[jax.ShapeDtypeStruct((n, LANES), I32), jax.ShapeDtypeStruct((n, LANES), F32),
                   jax.ShapeDtypeStruct((n, LANES), I32), jax.ShapeDtypeStruct((8, LANES), I32)],
        scratch_shapes=[pltpu.VMEM((1, LANES), F32)],
        compiler_params=_cparams(("arbitrary",)),
        name="route_topk",
    )(logits, bias_pad)


def _dispatch_kernel(zs_ref, dest_ref, x_hbm, xs_hbm, zero_ref, sem, zsem, *, bt, n_experts):
    i = pl.program_id(0)
    d = x_hbm.shape[1]

    def zero_copy(e):
        zs = pl.multiple_of(zs_ref[e], SUB_ROWS)
        return pltpu.make_async_copy(zero_ref, xs_hbm.at[pl.ds(zs, SUB_ROWS), :], zsem)

    @pl.when(i == 0)
    def _():
        zero_ref[...] = jnp.zeros_like(zero_ref)

        def zstart(e, c):
            zero_copy(e).start()
            return c

        def zwait(e, c):
            zero_copy(e).wait()
            return c

        lax.fori_loop(0, n_experts, zstart, 0)
        lax.fori_loop(0, n_experts, zwait, 0)

    def row_copy(t, k):
        dst = dest_ref[0, 0, t * TOP_K + k]
        return pltpu.make_async_copy(x_hbm.at[pl.ds(i * bt + t, 1), :],
                                     xs_hbm.at[pl.ds(dst, 1), :], sem)

    def start(t, c):
        for k in range(TOP_K):
            row_copy(t, k).start()
        return c

    def wait(t, c):
        for k in range(TOP_K):
            row_copy(t, k).wait()
        return c

    lax.fori_loop(0, bt, start, 0)
    lax.fori_loop(0, bt, wait, 0)
    del d


def _dispatch(zstart, dest3, x1, *, rows_total, n_experts, bt=TOK_TILE):
    n, d = x1.shape
    grid_spec = pltpu.PrefetchScalarGridSpec(
        num_scalar_prefetch=1,
        grid=(n // bt,),
        in_specs=[pl.BlockSpec((1, 1, bt * TOP_K), lambda i, zs: (i, 0, 0),
                               memory_space=pltpu.SMEM),
                  pl.BlockSpec(memory_space=pl.ANY)],
        out_specs=pl.BlockSpec(memory_space=pl.ANY),
        scratch_shapes=[pltpu.VMEM((SUB_ROWS, d), F32),
                        pltpu.SemaphoreType.DMA(()),
                        pltpu.SemaphoreType.DMA(())],
    )
    return pl.pallas_call(
        functools.partial(_dispatch_kernel, bt=bt, n_experts=n_experts),
        grid_spec=grid_spec,
        out_shape=jax.ShapeDtypeStruct((rows_total, d), F32),
        compiler_params=_cparams(("arbitrary",)),
        name="dispatch_rows",
    )(zstart, dest3, x1)


def _experts_kernel(ce_ref, cv_ref, tot_ref, x_ref, w1_ref, w3_ref, w2_ref, o_ref,
                    w13_ref, acc_ref, h_ref, w2b_ref, *, n_up, fdim):
    c = pl.program_id(0)
    j = pl.program_id(1)
    nv = cv_ref[c]
    n_sub = ROW_CHUNK // SUB_ROWS
    del ce_ref, tot_ref

    @pl.when(j < n_up)
    def _():
        w13_ref[:, :fdim] = w1_ref[...].astype(BF16)
        w13_ref[:, fdim:] = w3_ref[...].astype(BF16)
        for sb in range(n_sub):
            rows = slice(sb * SUB_ROWS, (sb + 1) * SUB_ROWS)

            @pl.when(sb * SUB_ROWS < nv)
            def _():
                part = jnp.dot(x_ref[rows, :].astype(BF16), w13_ref[...],
                               preferred_element_type=F32)

                @pl.when(j == 0)
                def _():
                    acc_ref[rows, :] = part

                @pl.when(j > 0)
                def _():
                    acc_ref[rows, :] += part

                @pl.when(j == n_up - 1)
                def _():
                    a = acc_ref[rows, :]
                    h1 = a[:, :fdim]
                    h_ref[rows, :] = (h1 * jax.nn.sigmoid(h1) * a[:, fdim:]).astype(BF16)

    @pl.when(j >= n_up)
    def _():
        w2b_ref[...] = w2_ref[...].astype(BF16)
        for sb in range(n_sub):
            rows = slice(sb * SUB_ROWS, (sb + 1) * SUB_ROWS)

            @pl.when(sb * SUB_ROWS < nv)
            def _():
                o_ref[rows, :] = jnp.dot(h_ref[rows, :], w2b_ref[...],
                                         preferred_element_type=F32)

            @pl.when(sb * SUB_ROWS >= nv)
            def _():
                o_ref[rows, :] = jnp.zeros((SUB_ROWS, o_ref.shape[1]), F32)


def _experts(chunk_expert, chunk_valid, total, xs, w1, w3, w2, *, n_chunks):
    d = xs.shape[1]
    fdim = w1.shape[2]
    n_up = d // UP_K_TILE
    n_down = d // DOWN_N_TILE

    def x_map(c, j, ce, cv, tot):
        return (jnp.minimum(c, tot[0] - 1), jnp.minimum(j, n_up - 1))

    def w13_map(c, j, ce, cv, tot):
        return (ce[c], jnp.minimum(j, n_up - 1), 0)

    def w2_map(c, j, ce, cv, tot):
        return (ce[c], 0, jnp.maximum(j - n_up, 0))

    def o_map(c, j, ce, cv, tot):
        return (c, jnp.maximum(j - n_up, 0))

    grid_spec = pltpu.PrefetchScalarGridSpec(
        num_scalar_prefetch=3,
        grid=(n_chunks, n_up + n_down),
        in_specs=[pl.BlockSpec((ROW_CHUNK, UP_K_TILE), x_map),
                  pl.BlockSpec((None, UP_K_TILE, fdim), w13_map),
                  pl.BlockSpec((None, UP_K_TILE, fdim), w13_map),
                  pl.BlockSpec((None, fdim, DOWN_N_TILE), w2_map)],
        out_specs=pl.BlockSpec((ROW_CHUNK, DOWN_N_TILE), o_map),
        scratch_shapes=[pltpu.VMEM((UP_K_TILE, 2 * fdim), BF16),
                        pltpu.VMEM((ROW_CHUNK, 2 * fdim), F32),
                        pltpu.VMEM((ROW_CHUNK, fdim), BF16),
                        pltpu.VMEM((fdim, DOWN_N_TILE), BF16)],
    )
    return pl.pallas_call(
        functools.partial(_experts_kernel, n_up=n_up, fdim=fdim),
        grid_spec=grid_spec,
        out_shape=jax.ShapeDtypeStruct((n_chunks * ROW_CHUNK, d), F32),
        compiler_params=_cparams(("arbitrary", "arbitrary")),
        name="routed_experts",
    )(chunk_expert, chunk_valid, total, xs, w1, w3, w2)


def _shared_kernel(x_ref, w13_ref, w2_ref, o_ref, *, tf):
    f = pl.program_id(1)
    r = jnp.dot(x_ref[...], w13_ref[...], preferred_element_type=F32)
    h1 = r[:, :tf]
    h = (h1 * jax.nn.sigmoid(h1) * r[:, tf:]).astype(BF16)
    y = jnp.dot(h, w2_ref[...], preferred_element_type=F32)

    @pl.when(f == 0)
    def _():
        o_ref[...] = y

    @pl.when(f > 0)
    def _():
        o_ref[...] += y


def _shared_expert(x1b, w13s, w2s, *, bm=1024, tf=SHARED_F_TILE):
    n, d = x1b.shape
    nf = w2s.shape[0] // tf
    return pl.pallas_call(
        functools.partial(_shared_kernel, tf=tf),
        grid=(n // bm, nf),
        in_specs=[pl.BlockSpec((bm, d), lambda i, f: (i, 0)),
                  pl.BlockSpec((d, 2 * tf), lambda i, f: (0, f)),
                  pl.BlockSpec((tf, d), lambda i, f: (f, 0))],
        out_specs=pl.BlockSpec((bm, d), lambda i, f: (i, 0)),
        out_shape=jax.ShapeDtypeStruct((n, d), F32),
        compiler_params=_cparams(("parallel", "arbitrary")),
        name="shared_expert",
    )(x1b, w13s, w2s)


def _combine_kernel(dest_ref, x1_ref, ysh_ref, gate_ref, g_ref, b_ref, ys_hbm, o_ref,
                    buf_ref, sem, *, bt):
    def row_copy(t, k):
        src = dest_ref[0, 0, t * TOP_K + k]
        return pltpu.make_async_copy(ys_hbm.at[pl.ds(src, 1), :],
                                     buf_ref.at[k, pl.ds(t, 1), :], sem)

    def start(t, c):
        for k in range(TOP_K):
            row_copy(t, k).start()
        return c

    def wait(t, c):
        for k in range(TOP_K):
            row_copy(t, k).wait()
        return c

    lax.fori_loop(0, bt, start, 0)
    acc = ALPHA * x1_ref[...] + ysh_ref[...]
    lax.fori_loop(0, bt, wait, 0)
    gate = gate_ref[...]
    for k in range(TOP_K):
        acc = acc + gate[:, k:k + 1] * buf_ref[k]
    o_ref[...] = _layer_norm(acc, g_ref[...], b_ref[...])


def _combine(dest3, x1, ysh, gate, g2, b2, ys, *, bt=TOK_TILE):
    n, d = x1.shape
    row = lambda i: (i, 0)
    fix = lambda i: (0, 0)
    return pl.pallas_call(
        functools.partial(_combine_kernel, bt=bt),
        grid=(n // bt,),
        in_specs=[pl.BlockSpec((1, 1, bt * TOP_K), lambda i: (i, 0, 0), memory_space=pltpu.SMEM),
                  pl.BlockSpec((bt, d), row), pl.BlockSpec((bt, d), row),
                  pl.BlockSpec((bt, LANES), row),
                  pl.BlockSpec((1, d), fix), pl.BlockSpec((1, d), fix),
                  pl.BlockSpec(memory_space=pl.ANY)],
        out_specs=pl.BlockSpec((bt, d), row),
        out_shape=jax.ShapeDtypeStruct((n, d), F32),
        scratch_shapes=[pltpu.VMEM((TOP_K, bt, d), F32), pltpu.SemaphoreType.DMA(())],
        compiler_params=_cparams(("arbitrary",)),
        name="combine_ln2",
    )(dest3, x1, ysh, gate, g2, b2, ys)


def _layer(x, w_in, b_forget, pool_w, pool_scale, w_branch_pool, w_branch_attn, w_out,
           ln1_g, ln1_b, w_router, router_bias, w1, w3, w2, w_shared1, w_shared3, w_shared2,
           ln2_g, ln2_b):
    batch, seq, d = x.shape
    n = batch * seq
    pool_width = w_branch_pool.shape[0]
    attn_width = w_branch_attn.shape[0]
    n_heads = attn_width // HEAD_DIM
    n_experts = w1.shape[0]
    off_q = pool_width
    off_k = off_q + attn_width
    off_v = off_k + attn_width
    off_f = off_v + attn_width
    off_gate = off_f + n_heads

    x2 = x.reshape(n, d)

    w_main = w_in[:, :off_f].astype(BF16)
    wf_pad = jnp.pad(w_in[:, off_f:off_gate], ((0, 0), (0, LANES - n_heads))).astype(BF16)
    bf_pad = jnp.pad(b_forget.astype(F32), (0, LANES - n_heads)).reshape(1, LANES)
    w_gate = w_in[:, off_gate:].astype(BF16)
    piece = jnp.arange(3)[:, None, None]
    src = jnp.arange(LANES)[None, :, None]
    dst = jnp.arange(n_heads * LANES)[None, None, :]
    sel = ((src < n_heads) & (dst == src * LANES + piece)).astype(BF16)
    wr = jnp.pad(w_router.astype(F32), ((0, 0), (0, LANES - n_experts)))
    wr_hi = wr.astype(BF16)
    wr_lo = (wr - wr_hi.astype(F32)).astype(BF16)
    rb_pad = jnp.pad(router_bias.astype(F32), (0, LANES - n_experts)).reshape(1, LANES)
    tf = SHARED_F_TILE
    fs = w_shared1.shape[1]
    w13s = jnp.concatenate([w_shared1.reshape(d, fs // tf, tf), w_shared3.reshape(d, fs // tf, tf)],
                           axis=2).reshape(d, 2 * fs).astype(BF16)

    z = _inproj(x2, w_main, q_lo=off_q, q_hi=off_k, q_scale=HEAD_DIM ** -0.5)
    ka = _forget_bias(x2, wf_pad, bf_pad, sel, batch=batch, seq=seq, n_heads=n_heads)
    pool_o = _pool(z, pool_w.astype(BF16), pool_scale.astype(F32).reshape(1, pool_width),
                   batch=batch, seq=seq, width=pool_width)
    attn_o = _attention(z, ka, batch=batch, seq=seq, n_heads=n_heads,
                        q_col=off_q // HEAD_DIM, k_col=off_k // HEAD_DIM, v_col=off_v // HEAD_DIM)
    merged = _merge(x2, pool_o, attn_o, w_gate, w_branch_pool.astype(BF16),
                    w_branch_attn.astype(BF16))
    x1, x1b, logits = _out_ln_router(x2, merged, w_out.astype(BF16),
                                     ln1_g.astype(F32).reshape(1, d), ln1_b.astype(F32).reshape(1, d),
                                     wr_hi, wr_lo)

    idx, gate, rank, cnt = _route(logits, rb_pad, n_experts=n_experts)
    counts = cnt[0, :n_experts]
    n_assign = n * TOP_K
    n_chunks = -(-n_assign // ROW_CHUNK) + n_experts
    chunks_per = (counts + ROW_CHUNK - 1) // ROW_CHUNK
    chunk_end = jnp.cumsum(chunks_per)
    chunk_begin = chunk_end - chunks_per
    total = chunk_end[-1]
    group_start = chunk_begin * ROW_CHUNK
    cids = jnp.arange(n_chunks, dtype=I32)
    c_eff = jnp.minimum(cids, total - 1)
    chunk_expert = jnp.minimum(jnp.searchsorted(chunk_end, c_eff, side="right"), n_experts - 1).astype(I32)
    chunk_valid = jnp.where(cids < total,
                            jnp.minimum(counts[chunk_expert] - (cids - chunk_begin[chunk_expert]) * ROW_CHUNK,
                                        ROW_CHUNK), 0).astype(I32)
    idx_k = idx[:, :TOP_K]
    dest = group_start[idx_k].astype(I32) + rank[:, :TOP_K]
    dest3 = dest.reshape(n // TOK_TILE, 1, TOK_TILE * TOP_K)
    zstart = ((group_start + counts) // SUB_ROWS * SUB_ROWS).astype(I32)
    rows_total = (n_chunks + 1) * ROW_CHUNK

    xs = _dispatch(zstart, dest3, x1, rows_total=rows_total, n_experts=n_experts)
    ys = _experts(chunk_expert, chunk_valid, total.reshape(1).astype(I32), xs, w1, w3, w2,
                  n_chunks=n_chunks)
    ysh = _shared_expert(x1b, w13s, w_shared2.astype(BF16))
    out = _combine(dest3, x1, ysh, gate, ln2_g.astype(F32).reshape(1, d),
                   ln2_b.astype(F32).reshape(1, d), ys)
    return out.reshape(batch, seq, d)


def kernel(x, w_in, b_forget, pool_w, pool_scale, w_branch_pool, w_branch_attn, w_out, ln1_g, ln1_b,
           w_router, router_bias, w1, w3, w2, w_shared1, w_shared3, w_shared2, ln2_g, ln2_b):
    for l in range(w_in.shape[0]):
        x = _layer(x, w_in[l], b_forget[l], pool_w[l], pool_scale[l], w_branch_pool[l],
                   w_branch_attn[l], w_out[l], ln1_g[l], ln1_b[l], w_router[l], router_bias[l],
                   w1[l], w3[l], w2[l], w_shared1[l], w_shared3[l], w_shared2[l], ln2_g[l], ln2_b[l])
    return x
```

```python
import functools

import jax
import jax.numpy as jnp
from jax import lax
from jax.experimental import pallas as pl
from jax.experimental.pallas import tpu as pltpu

F32 = jnp.float32
BF16 = jnp.bfloat16
I32 = jnp.int32

POOL_WINDOWS = (2, 4, 8, 16)
HEAD_DIM = 128
TOP_K = 6
ROUTE_SCALE = 1.0
DEPTH = 1
ALPHA = (2 * DEPTH) ** 0.25
LN_EPS = 1e-5

LANES = 128
MAX_WINDOW = max(POOL_WINDOWS)
ATTN_BLOCK = 512
Q_HALVES = 2
ROW_CHUNK = 1024
SUB_ROWS = 256
UP_K_TILE = 512
DOWN_N_TILE = 512
SHARED_F_TILE = 256
TOK_TILE = 256
VMEM_LIMIT = 56 * 1024 * 1024


def _cparams(sem, vmem=VMEM_LIMIT):
    return pltpu.CompilerParams(dimension_semantics=sem, vmem_limit_bytes=vmem)


def _split3(v):
    hi = v.astype(BF16)
    r1 = v - hi.astype(F32)
    mid = r1.astype(BF16)
    lo = (r1 - mid.astype(F32)).astype(BF16)
    return hi, mid, lo


def _layer_norm(v, g, b):
    mu = jnp.mean(v, axis=-1, keepdims=True)
    d = v - mu
    var = jnp.mean(d * d, axis=-1, keepdims=True)
    return d * lax.rsqrt(var + LN_EPS) * g + b


def _inproj_kernel(x_ref, w_ref, o_ref, xb_ref, *, bn, q_lo, q_hi, q_scale):
    j = pl.program_id(1)

    @pl.when(j == 0)
    def _():
        xb_ref[...] = x_ref[...].astype(BF16)

    acc = jnp.dot(xb_ref[...], w_ref[...], preferred_element_type=F32)
    col0 = j * bn
    s = jnp.where((col0 >= q_lo) & (col0 < q_hi), q_scale, 1.0).astype(F32)
    o_ref[...] = (acc * s).astype(BF16)


def _inproj(x2, w_main, *, q_lo, q_hi, q_scale, bm=1024, bn=512):
    n, d = x2.shape
    c = w_main.shape[1]
    return pl.pallas_call(
        functools.partial(_inproj_kernel, bn=bn, q_lo=q_lo, q_hi=q_hi, q_scale=q_scale),
        grid=(n // bm, c // bn),
        in_specs=[pl.BlockSpec((bm, d), lambda i, j: (i, 0)),
                  pl.BlockSpec((d, bn), lambda i, j: (0, j))],
        out_specs=pl.BlockSpec((bm, bn), lambda i, j: (i, j)),
        out_shape=jax.ShapeDtypeStruct((n, c), BF16),
        scratch_shapes=[pltpu.VMEM((bm, d), BF16)],
        compiler_params=_cparams(("parallel", "arbitrary")),
        name="inproj",
    )(x2, w_main)


def _forget_kernel(x_ref, wf_ref, b_ref, sel_ref, ka_ref, carry_ref, *, ts, n_heads):
    i = pl.program_id(1)

    @pl.when(i == 0)
    def _():
        carry_ref[...] = jnp.zeros_like(carry_ref)

    z = jnp.dot(x_ref[...].astype(BF16), wf_ref[...], preferred_element_type=F32) + b_ref[...]
    lf = jnp.minimum(z, 0.0) - jnp.log1p(jnp.exp(-jnp.abs(z)))
    lane = lax.broadcasted_iota(I32, lf.shape, 1)
    lf = jnp.where(lane < n_heads, lf, 0.0)

    row = lax.broadcasted_iota(I32, (ts, ts), 0)
    col = lax.broadcasted_iota(I32, (ts, ts), 1)
    tri = jnp.where(col <= row, 1.0, 0.0).astype(BF16)
    hi, mid, lo = _split3(lf)
    cs = (jnp.dot(tri, hi, preferred_element_type=F32)
          + jnp.dot(tri, mid, preferred_element_type=F32)
          + jnp.dot(tri, lo, preferred_element_type=F32)) + carry_ref[...]
    carry_ref[...] = cs[ts - 1:ts, :]

    h2, m2, l2 = _split3(-cs)
    ka = (jnp.dot(h2, sel_ref[0], preferred_element_type=F32)
          + jnp.dot(m2, sel_ref[1], preferred_element_type=F32)
          + jnp.dot(l2, sel_ref[2], preferred_element_type=F32))
    ka_ref[...] = ka.astype(BF16)


def _forget_bias(x2, wf_pad, b_pad, sel, *, batch, seq, n_heads, ts=512):
    n, d = x2.shape
    nt = seq // ts
    return pl.pallas_call(
        functools.partial(_forget_kernel, ts=ts, n_heads=n_heads),
        grid=(batch, nt),
        in_specs=[pl.BlockSpec((ts, d), lambda b, i: (b * nt + i, 0)),
                  pl.BlockSpec((d, LANES), lambda b, i: (0, 0)),
                  pl.BlockSpec((1, LANES), lambda b, i: (0, 0)),
                  pl.BlockSpec((3, LANES, n_heads * LANES), lambda b, i: (0, 0, 0))],
        out_specs=pl.BlockSpec((ts, n_heads * LANES), lambda b, i: (b * nt + i, 0)),
        out_shape=jax.ShapeDtypeStruct((n, n_heads * LANES), BF16),
        scratch_shapes=[pltpu.VMEM((1, LANES), F32)],
        compiler_params=_cparams(("arbitrary", "arbitrary")),
        name="forget_bias",
    )(x2, wf_pad, b_pad, sel)


def _pool_kernel(u_ref, halo_ref, pw_ref, sc_ref, o_ref, buf_ref, *, ts, gdim):
    i = pl.program_id(1)
    h = MAX_WINDOW
    halo = halo_ref[...].astype(F32)
    buf_ref[0:h, :] = jnp.where(i > 0, halo, 0.0)
    buf_ref[h:h + ts, :] = u_ref[...].astype(F32)

    t = i * ts + lax.broadcasted_iota(I32, (ts, 1), 0)
    for g, w in enumerate(POOL_WINDOWS):
        cs = slice(g * gdim, (g + 1) * gdim)
        tok = buf_ref[h:h + ts, cs]
        wsum = tok
        for k in range(1, w):
            wsum = wsum + buf_ref[h - k:h - k + ts, cs]
        cnt = jnp.minimum(t + 1, w).astype(F32)
        p = (wsum / cnt - tok).astype(BF16)
        y = jnp.dot(p, pw_ref[g], preferred_element_type=F32) * sc_ref[:, cs]
        o_ref[:, cs] = y.astype(BF16)


def _pool(z, pool_w_b, pool_scale2, *, batch, seq, width, ts=512):
    n = z.shape[0]
    groups = len(POOL_WINDOWS)
    gdim = width // groups
    nt = seq // ts
    hb = ts // MAX_WINDOW
    return pl.pallas_call(
        functools.partial(_pool_kernel, ts=ts, gdim=gdim),
        grid=(batch, nt),
        in_specs=[pl.BlockSpec((ts, width), lambda b, i: (b * nt + i, 0)),
                  pl.BlockSpec((MAX_WINDOW, width),
                               lambda b, i: (jnp.maximum((b * nt + i) * hb - 1, 0), 0)),
                  pl.BlockSpec((groups, gdim, gdim), lambda b, i: (0, 0, 0)),
                  pl.BlockSpec((1, width), lambda b, i: (0, 0))],
        out_specs=pl.BlockSpec((ts, width), lambda b, i: (b * nt + i, 0)),
        out_shape=jax.ShapeDtypeStruct((n, width), BF16),
        scratch_shapes=[pltpu.VMEM((ts + MAX_WINDOW, width), F32)],
        compiler_params=_cparams(("parallel", "parallel")),
        name="pool_mixer",
    )(z, z, pool_w_b, pool_scale2)


def _attn_kernel(q_ref, k_ref, ka_ref, v_ref, o_ref, qa_ref, m_ref, l_ref, acc_ref, *, blk):
    qi = pl.program_id(2)
    lane = lax.broadcasted_iota(I32, (blk, HEAD_DIM), 1)
    ones3 = jnp.where(lane < 3, 1.0, 0.0).astype(BF16)
    for h in range(Q_HALVES):
        qa_ref[h] = jnp.concatenate([q_ref[h * blk:(h + 1) * blk, :], ones3], axis=1)
    m_ref[...] = jnp.full_like(m_ref, -jnp.inf)
    l_ref[...] = jnp.zeros_like(l_ref)
    acc_ref[...] = jnp.zeros_like(acc_ref)
    reps = blk // LANES

    def step(h, ki, masked):
        ks = pl.multiple_of(ki * blk, blk)
        k_aug = jnp.concatenate([k_ref[pl.ds(ks, blk), :], ka_ref[pl.ds(ks, blk), :]], axis=1)
        s = lax.dot_general(qa_ref[h], k_aug, (((1,), (1,)), ((), ())),
                            preferred_element_type=F32)
        if masked:
            r = lax.broadcasted_iota(I32, (blk, blk), 0)
            c = lax.broadcasted_iota(I32, (blk, blk), 1)
            s = jnp.where(c <= r, s, -jnp.inf)
        m_prev = m_ref[h]
        m_next = jnp.maximum(m_prev, jnp.max(s, axis=1, keepdims=True))
        alpha = jnp.exp(m_prev - m_next)
        p = jnp.exp(s - jnp.concatenate([m_next] * reps, axis=1))
        l_ref[h] = alpha * l_ref[h] + jnp.sum(p, axis=1, keepdims=True)
        acc_ref[h] = alpha * acc_ref[h] + jnp.dot(
            p.astype(BF16), v_ref[pl.ds(ks, blk), :], preferred_element_type=F32)
        m_ref[h] = m_next

    def body(ki, carry):
        for h in range(Q_HALVES):
            step(h, ki, False)
        return carry

    lax.fori_loop(0, Q_HALVES * qi, body, 0)
    for h in range(Q_HALVES):
        for d in range(h):
            step(h, Q_HALVES * qi + d, False)
        step(h, Q_HALVES * qi + h, True)
        o_ref[h * blk:(h + 1) * blk, :] = (acc_ref[h] / l_ref[h]).astype(BF16)


def _attention(z, ka, *, batch, seq, n_heads, q_col, k_col, v_col, blk=ATTN_BLOCK):
    n = z.shape[0]
    qb = Q_HALVES * blk
    nq = seq // qb
    return pl.pallas_call(
        functools.partial(_attn_kernel, blk=blk),
        grid=(batch, n_heads, nq),
        in_specs=[pl.BlockSpec((qb, HEAD_DIM), lambda b, h, i: (b * nq + i, q_col + h)),
                  pl.BlockSpec((seq, HEAD_DIM), lambda b, h, i: (b, k_col + h)),
                  pl.BlockSpec((seq, HEAD_DIM), lambda b, h, i: (b, h)),
                  pl.BlockSpec((seq, HEAD_DIM), lambda b, h, i: (b, v_col + h))],
        out_specs=pl.BlockSpec((qb, HEAD_DIM), lambda b, h, i: (b * nq + i, h)),
        out_shape=jax.ShapeDtypeStruct((n, n_heads * HEAD_DIM), BF16),
        scratch_shapes=[pltpu.VMEM((Q_HALVES, blk, 2 * HEAD_DIM), BF16),
                        pltpu.VMEM((Q_HALVES, blk, LANES), F32),
                        pltpu.VMEM((Q_HALVES, blk, LANES), F32),
                        pltpu.VMEM((Q_HALVES, blk, HEAD_DIM), F32)],
        compiler_params=_cparams(("parallel", "parallel", "arbitrary")),
        name="fox_attention",
    )(z, z, ka, z)


def _merge_kernel(x_ref, p_ref, a_ref, wg0_ref, wg1_ref, wbp_ref, wba_ref, o_ref, xb_ref):
    j = pl.program_id(1)

    @pl.when(j == 0)
    def _():
        xb_ref[...] = x_ref[...].astype(BF16)

    xb = xb_ref[...]
    g0 = jax.nn.sigmoid(jnp.dot(xb, wg0_ref[...], preferred_element_type=F32))
    g1 = jax.nn.sigmoid(jnp.dot(xb, wg1_ref[...], preferred_element_type=F32))
    yp = jnp.dot(p_ref[...], wbp_ref[...], preferred_element_type=F32)
    ya = jnp.dot(a_ref[...], wba_ref[...], preferred_element_type=F32)
    o_ref[...] = (g0 * yp + g1 * ya).astype(BF16)


def _merge(x2, pool_o, attn_o, wg, wbp, wba, *, bm=512, bn=512):
    n, d = x2.shape
    wp = pool_o.shape[1]
    wa = attn_o.shape[1]
    nj = d // bn
    return pl.pallas_call(
        _merge_kernel,
        grid=(n // bm, nj),
        in_specs=[pl.BlockSpec((bm, d), lambda i, j: (i, 0)),
                  pl.BlockSpec((bm, wp), lambda i, j: (i, 0)),
                  pl.BlockSpec((bm, wa), lambda i, j: (i, 0)),
                  pl.BlockSpec((d, bn), lambda i, j: (0, j)),
                  pl.BlockSpec((d, bn), lambda i, j: (0, nj + j)),
                  pl.BlockSpec((wp, bn), lambda i, j: (0, j)),
                  pl.BlockSpec((wa, bn), lambda i, j: (0, j))],
        out_specs=pl.BlockSpec((bm, bn), lambda i, j: (i, j)),
        out_shape=jax.ShapeDtypeStruct((n, d), BF16),
        scratch_shapes=[pltpu.VMEM((bm, d), BF16)],
        compiler_params=_cparams(("parallel", "arbitrary")),
        name="branch_merge",
    )(x2, pool_o, attn_o, wg, wg, wbp, wba)


def _out_ln_router_kernel(x_ref, m_ref, wo_ref, g_ref, b_ref, wrh_ref, wrl_ref,
                          x1_ref, x1b_ref, lg_ref):
    mix = jnp.dot(m_ref[...], wo_ref[...], preferred_element_type=F32)
    x1 = _layer_norm(ALPHA * x_ref[...] + mix, g_ref[...], b_ref[...])
    x1_ref[...] = x1
    hi = x1.astype(BF16)
    x1b_ref[...] = hi
    lo = (x1 - hi.astype(F32)).astype(BF16)
    lg_ref[...] = (jnp.dot(hi, wrh_ref[...], preferred_element_type=F32)
                   + jnp.dot(lo, wrh_ref[...], preferred_element_type=F32)
                   + jnp.dot(hi, wrl_ref[...], preferred_element_type=F32))


def _out_ln_router(x2, merged, wo, g1, b1, wr_hi, wr_lo, *, bm=512):
    n, d = x2.shape
    row = lambda i: (i, 0)
    fix = lambda i: (0, 0)
    return pl.pallas_call(
        _out_ln_router_kernel,
        grid=(n // bm,),
        in_specs=[pl.BlockSpec((bm, d), row), pl.BlockSpec((bm, d), row),
                  pl.BlockSpec((d, d), fix), pl.BlockSpec((1, d), fix), pl.BlockSpec((1, d), fix),
                  pl.BlockSpec((d, LANES), fix), pl.BlockSpec((d, LANES), fix)],
        out_specs=[pl.BlockSpec((bm, d), row), pl.BlockSpec((bm, d), row),
                   pl.BlockSpec((bm, LANES), row)],
        out_shape=[jax.ShapeDtypeStruct((n, d), F32), jax.ShapeDtypeStruct((n, d), BF16),
                   jax.ShapeDtypeStruct((n, LANES), F32)],
        compiler_params=_cparams(("parallel",)),
        name="outproj_ln1_router",
    )(x2, merged, wo, g1, b1, wr_hi, wr_lo)


def _route_kernel(lg_ref, bias_ref, idx_ref, gate_ref, rank_ref, cnt_ref, carry_ref,
                  *, bm, n_experts):
    i = pl.program_id(0)

    @pl.when(i == 0)
    def _():
        carry_ref[...] = jnp.zeros_like(carry_ref)

    lane = lax.broadcasted_iota(I32, (bm, LANES), 1)
    lane_f = lane.astype(F32)
    scores = jax.nn.sigmoid(lg_ref[...])
    sel = jnp.where(lane < n_experts, scores + bias_ref[...], -jnp.inf)

    hits, gates = [], []
    gsum = jnp.zeros((bm, 1), F32)
    member = jnp.zeros((bm, LANES), F32)
    idx_out = jnp.zeros((bm, LANES), F32)
    for k in range(TOP_K):
        m = jnp.max(sel, axis=1, keepdims=True)
        ik = jnp.min(jnp.where(sel == m, lane_f, float(LANES)), axis=1, keepdims=True)
        hit = lane_f == ik
        gk = jnp.sum(jnp.where(hit, scores, 0.0), axis=1, keepdims=True)
        sel = jnp.where(hit, -jnp.inf, sel)
        member = jnp.where(hit, 1.0, member)
        idx_out = jnp.where(lane == k, ik, idx_out)
        gsum = gsum + gk
        hits.append(hit)
        gates.append(gk)

    row = lax.broadcasted_iota(I32, (bm, bm), 0)
    col = lax.broadcasted_iota(I32, (bm, bm), 1)
    before = jnp.where(col < row, 1.0, 0.0).astype(BF16)
    prefix = jnp.dot(before, member.astype(BF16), preferred_element_type=F32) + carry_ref[...]

    gate_out = jnp.zeros((bm, LANES), F32)
    rank_out = jnp.zeros((bm, LANES), F32)
    for k in range(TOP_K):
        rk = jnp.sum(jnp.where(hits[k], prefix, 0.0), axis=1, keepdims=True)
        rank_out = jnp.where(lane == k, rk, rank_out)
        gate_out = jnp.where(lane == k, gates[k] / gsum * ROUTE_SCALE, gate_out)

    total = carry_ref[...] + jnp.sum(member, axis=0, keepdims=True)
    carry_ref[...] = total
    idx_ref[...] = idx_out.astype(I32)
    gate_ref[...] = gate_out
    rank_ref[...] = rank_out.astype(I32)
    cnt_ref[...] = jnp.broadcast_to(total, cnt_ref.shape).astype(I32)


def _route(logits, bias_pad, *, n_experts, bm=TOK_TILE):
    n = logits.shape[0]
    row = lambda i: (i, 0)
    fix = lambda i: (0, 0)
    return pl.pallas_call(
        functools.partial(_route_kernel, bm=bm, n_experts=n_experts),
        grid=(n // bm,),
        in_specs=[pl.BlockSpec((bm, LANES), row), pl.BlockSpec((1, LANES), fix)],
        out_specs=[pl.BlockSpec((bm, LANES), row), pl.BlockSpec((bm, LANES), row),
                   pl.BlockSpec((bm, LANES), row), pl.BlockSpec((8, LANES), fix)],
        out_shape=[jax.ShapeDtypeStruct((n, LANES), I32), jax.ShapeDtypeStruct((n, LANES), F32),
                   jax.ShapeDtypeStruct((n, LANES), I32), jax.ShapeDtypeStruct((8, LANES), I32)],
        scratch_shapes=[pltpu.VMEM((1, LANES), F32)],
        compiler_params=_cparams(("arbitrary",)),
        name="route_topk",
    )(logits, bias_pad)


def _dispatch_kernel(zs_ref, dest_ref, x_ref, xs_hbm, zero_ref, sem, zsem, *, bt, n_experts):
    i = pl.program_id(0)

    def zero_copy(e):
        zs = pl.multiple_of(zs_ref[e], SUB_ROWS)
        return pltpu.make_async_copy(zero_ref, xs_hbm.at[pl.ds(zs, SUB_ROWS), :], zsem)

    @pl.when(i == 0)
    def _():
        zero_ref[...] = jnp.zeros_like(zero_ref)

        def zstart(e, c):
            zero_copy(e).start()
            return c

        def zwait(e, c):
            zero_copy(e).wait()
            return c

        lax.fori_loop(0, n_experts, zstart, 0)
        lax.fori_loop(0, n_experts, zwait, 0)

    def row_copy(t, k):
        dst = dest_ref[0, 0, t * TOP_K + k]
        return pltpu.make_async_copy(x_ref.at[pl.ds(t, 1), :], xs_hbm.at[pl.ds(dst, 1), :], sem)

    def start(t, c):
        for k in range(TOP_K):
            row_copy(t, k).start()
        return c

    def wait(t, c):
        for k in range(TOP_K):
            row_copy(t, k).wait()
        return c

    lax.fori_loop(0, bt, start, 0)
    lax.fori_loop(0, bt, wait, 0)


def _dispatch(zstart, dest3, x1, *, rows_total, n_experts, bt=TOK_TILE):
    n, d = x1.shape
    grid_spec = pltpu.PrefetchScalarGridSpec(
        num_scalar_prefetch=1,
        grid=(n // bt,),
        in_specs=[pl.BlockSpec((1, 1, bt * TOP_K), lambda i, zs: (i, 0, 0),
                               memory_space=pltpu.SMEM),
                  pl.BlockSpec((bt, d), lambda i, zs: (i, 0))],
        out_specs=pl.BlockSpec(memory_space=pl.ANY),
        scratch_shapes=[pltpu.VMEM((SUB_ROWS, d), F32),
                        pltpu.SemaphoreType.DMA(()),
                        pltpu.SemaphoreType.DMA(())],
    )
    return pl.pallas_call(
        functools.partial(_dispatch_kernel, bt=bt, n_experts=n_experts),
        grid_spec=grid_spec,
        out_shape=jax.ShapeDtypeStruct((rows_total, d), F32),
        compiler_params=_cparams(("arbitrary",)),
        name="dispatch_rows",
    )(zstart, dest3, x1)


def _experts_kernel(ce_ref, cv_ref, tot_ref, x_ref, w1_ref, w3_ref, w2_ref, o_ref,
                    w13_ref, acc_ref, h_ref, w2b_ref, *, n_up, fdim):
    c = pl.program_id(0)
    j = pl.program_id(1)
    nv = cv_ref[c]
    del ce_ref, tot_ref
    sub_blocks = [slice(sb * SUB_ROWS, (sb + 1) * SUB_ROWS) for sb in range(ROW_CHUNK // SUB_ROWS)]

    def up(first):
        w13_ref[:, :fdim] = w1_ref[...].astype(BF16)
        w13_ref[:, fdim:] = w3_ref[...].astype(BF16)
        for rows in sub_blocks:
            @pl.when(rows.start < nv)
            def _():
                part = jnp.dot(x_ref[rows, :].astype(BF16), w13_ref[...],
                               preferred_element_type=F32)
                if first:
                    acc_ref[rows, :] = part
                else:
                    acc_ref[rows, :] += part

    @pl.when(j == 0)
    def _():
        up(True)

    @pl.when((j > 0) & (j < n_up))
    def _():
        up(False)

    @pl.when(j == n_up - 1)
    def _():
        for rows in sub_blocks:
            @pl.when(rows.start < nv)
            def _():
                h1 = acc_ref[rows, :fdim]
                h_ref[rows, :] = (h1 * jax.nn.sigmoid(h1) * acc_ref[rows, fdim:]).astype(BF16)

    @pl.when(j >= n_up)
    def _():
        w2b_ref[...] = w2_ref[...].astype(BF16)
        for rows in sub_blocks:
            @pl.when(rows.start < nv)
            def _():
                o_ref[rows, :] = jnp.dot(h_ref[rows, :], w2b_ref[...],
                                         preferred_element_type=F32)

            @pl.when(rows.start >= nv)
            def _():
                o_ref[rows, :] = jnp.zeros((SUB_ROWS, o_ref.shape[1]), F32)


def _experts(chunk_expert, chunk_valid, total, xs, w1, w3, w2, *, n_chunks):
    d = xs.shape[1]
    fdim = w1.shape[2]
    n_up = d // UP_K_TILE
    n_down = d // DOWN_N_TILE

    def x_map(c, j, ce, cv, tot):
        return (jnp.minimum(c, tot[0] - 1), jnp.minimum(j, n_up - 1))

    def w13_map(c, j, ce, cv, tot):
        return (ce[c], jnp.minimum(j, n_up - 1), 0)

    def w2_map(c, j, ce, cv, tot):
        return (ce[c], 0, jnp.maximum(j - n_up, 0))

    def o_map(c, j, ce, cv, tot):
        return (c, jnp.maximum(j - n_up, 0))

    grid_spec = pltpu.PrefetchScalarGridSpec(
        num_scalar_prefetch=3,
        grid=(n_chunks, n_up + n_down),
        in_specs=[pl.BlockSpec((ROW_CHUNK, UP_K_TILE), x_map),
                  pl.BlockSpec((None, UP_K_TILE, fdim), w13_map),
                  pl.BlockSpec((None, UP_K_TILE, fdim), w13_map),
                  pl.BlockSpec((None, fdim, DOWN_N_TILE), w2_map)],
        out_specs=pl.BlockSpec((ROW_CHUNK, DOWN_N_TILE), o_map),
        scratch_shapes=[pltpu.VMEM((UP_K_TILE, 2 * fdim), BF16),
                        pltpu.VMEM((ROW_CHUNK, 2 * fdim), F32),
                        pltpu.VMEM((ROW_CHUNK, fdim), BF16),
                        pltpu.VMEM((fdim, DOWN_N_TILE), BF16)],
    )
    return pl.pallas_call(
        functools.partial(_experts_kernel, n_up=n_up, fdim=fdim),
        grid_spec=grid_spec,
        out_shape=jax.ShapeDtypeStruct((n_chunks * ROW_CHUNK, d), F32),
        compiler_params=_cparams(("arbitrary", "arbitrary")),
        name="routed_experts",
    )(chunk_expert, chunk_valid, total, xs, w1, w3, w2)


def _shared_kernel(x_ref, w13_ref, w2_ref, o_ref, *, tf):
    f = pl.program_id(1)
    r = jnp.dot(x_ref[...], w13_ref[...], preferred_element_type=F32)
    h1 = r[:, :tf]
    h = (h1 * jax.nn.sigmoid(h1) * r[:, tf:]).astype(BF16)
    y = jnp.dot(h, w2_ref[...], preferred_element_type=F32)

    @pl.when(f == 0)
    def _():
        o_ref[...] = y

    @pl.when(f > 0)
    def _():
        o_ref[...] += y


def _shared_expert(x1b, w13s, w2s, *, bm=1024, tf=SHARED_F_TILE):
    n, d = x1b.shape
    nf = w2s.shape[0] // tf
    return pl.pallas_call(
        functools.partial(_shared_kernel, tf=tf),
        grid=(n // bm, nf),
        in_specs=[pl.BlockSpec((bm, d), lambda i, f: (i, 0)),
                  pl.BlockSpec((d, 2 * tf), lambda i, f: (0, f)),
                  pl.BlockSpec((tf, d), lambda i, f: (f, 0))],
        out_specs=pl.BlockSpec((bm, d), lambda i, f: (i, 0)),
        out_shape=jax.ShapeDtypeStruct((n, d), F32),
        compiler_params=_cparams(("parallel", "arbitrary")),
        name="shared_expert",
    )(x1b, w13s, w2s)


def _combine_kernel(dest_ref, x1_ref, ysh_ref, gate_ref, g_ref, b_ref, ys_hbm, o_ref,
                    buf_ref, sem, *, bt):
    def row_copy(t, k):
        src = dest_ref[0, 0, t * TOP_K + k]
        return pltpu.make_async_copy(ys_hbm.at[pl.ds(src, 1), :],
                                     buf_ref.at[k, pl.ds(t, 1), :], sem)

    def start(t, c):
        for k in range(TOP_K):
            row_copy(t, k).start()
        return c

    def wait(t, c):
        for k in range(TOP_K):
            row_copy(t, k).wait()
        return c

    lax.fori_loop(0, bt, start, 0)
    acc = ALPHA * x1_ref[...] + ysh_ref[...]
    lax.fori_loop(0, bt, wait, 0)
    gate = gate_ref[...]
    for k in range(TOP_K):
        acc = acc + gate[:, k:k + 1] * buf_ref[k]
    o_ref[...] = _layer_norm(acc, g_ref[...], b_ref[...])


def _combine(dest3, x1, ysh, gate, g2, b2, ys, *, bt=TOK_TILE):
    n, d = x1.shape
    row = lambda i: (i, 0)
    fix = lambda i: (0, 0)
    return pl.pallas_call(
        functools.partial(_combine_kernel, bt=bt),
        grid=(n // bt,),
        in_specs=[pl.BlockSpec((1, 1, bt * TOP_K), lambda i: (i, 0, 0), memory_space=pltpu.SMEM),
                  pl.BlockSpec((bt, d), row), pl.BlockSpec((bt, d), row),
                  pl.BlockSpec((bt, LANES), row),
                  pl.BlockSpec((1, d), fix), pl.BlockSpec((1, d), fix),
                  pl.BlockSpec(memory_space=pl.ANY)],
        out_specs=pl.BlockSpec((bt, d), row),
        out_shape=jax.ShapeDtypeStruct((n, d), F32),
        scratch_shapes=[pltpu.VMEM((TOP_K, bt, d), F32), pltpu.SemaphoreType.DMA(())],
        compiler_params=_cparams(("arbitrary",)),
        name="combine_ln2",
    )(dest3, x1, ysh, gate, g2, b2, ys)


def _layer(x, w_in, b_forget, pool_w, pool_scale, w_branch_pool, w_branch_attn, w_out,
           ln1_g, ln1_b, w_router, router_bias, w1, w3, w2, w_shared1, w_shared3, w_shared2,
           ln2_g, ln2_b):
    batch, seq, d = x.shape
    n = batch * seq
    pool_width = w_branch_pool.shape[0]
    attn_width = w_branch_attn.shape[0]
    n_heads = attn_width // HEAD_DIM
    n_experts = w1.shape[0]
    off_q = pool_width
    off_k = off_q + attn_width
    off_v = off_k + attn_width
    off_f = off_v + attn_width
    off_gate = off_f + n_heads

    x2 = x.reshape(n, d)

    w_main = w_in[:, :off_f].astype(BF16)
    wf_pad = jnp.pad(w_in[:, off_f:off_gate], ((0, 0), (0, LANES - n_heads))).astype(BF16)
    bf_pad = jnp.pad(b_forget.astype(F32), (0, LANES - n_heads)).reshape(1, LANES)
    w_gate = w_in[:, off_gate:].astype(BF16)
    piece = jnp.arange(3)[:, None, None]
    src = jnp.arange(LANES)[None, :, None]
    dst = jnp.arange(n_heads * LANES)[None, None, :]
    sel = ((src < n_heads) & (dst == src * LANES + piece)).astype(BF16)
    wr = jnp.pad(w_router.astype(F32), ((0, 0), (0, LANES - n_experts)))
    wr_hi = wr.astype(BF16)
    wr_lo = (wr - wr_hi.astype(F32)).astype(BF16)
    rb_pad = jnp.pad(router_bias.astype(F32), (0, LANES - n_experts)).reshape(1, LANES)
    tf = SHARED_F_TILE
    fs = w_shared1.shape[1]
    w13s = jnp.concatenate([w_shared1.reshape(d, fs // tf, tf), w_shared3.reshape(d, fs // tf, tf)],
                           axis=2).reshape(d, 2 * fs).astype(BF16)

    z = _inproj(x2, w_main, q_lo=off_q, q_hi=off_k, q_scale=HEAD_DIM ** -0.5)
    ka = _forget_bias(x2, wf_pad, bf_pad, sel, batch=batch, seq=seq, n_heads=n_heads)
    pool_o = _pool(z, pool_w.astype(BF16), pool_scale.astype(F32).reshape(1, pool_width),
                   batch=batch, seq=seq, width=pool_width)
    attn_o = _attention(z, ka, batch=batch, seq=seq, n_heads=n_heads,
                        q_col=off_q // HEAD_DIM, k_col=off_k // HEAD_DIM, v_col=off_v // HEAD_DIM)
    merged = _merge(x2, pool_o, attn_o, w_gate, w_branch_pool.astype(BF16),
                    w_branch_attn.astype(BF16))
    x1, x1b, logits = _out_ln_router(x2, merged, w_out.astype(BF16),
                                     ln1_g.astype(F32).reshape(1, d), ln1_b.astype(F32).reshape(1, d),
                                     wr_hi, wr_lo)

    idx, gate, rank, cnt = _route(logits, rb_pad, n_experts=n_experts)
    counts = cnt[0, :n_experts]
    n_assign = n * TOP_K
    n_chunks = -(-n_assign // ROW_CHUNK) + n_experts
    chunks_per = (counts + ROW_CHUNK - 1) // ROW_CHUNK
    chunk_end = jnp.cumsum(chunks_per)
    chunk_begin = chunk_end - chunks_per
    total = chunk_end[-1]
    group_start = chunk_begin * ROW_CHUNK
    cids = jnp.arange(n_chunks, dtype=I32)
    c_eff = jnp.minimum(cids, total - 1)
    chunk_expert = jnp.minimum(jnp.sum((chunk_end[None, :] <= c_eff[:, None]).astype(I32), axis=1),
                               n_experts - 1).astype(I32)
    chunk_valid = jnp.where(cids < total,
                            jnp.minimum(counts[chunk_expert] - (cids - chunk_begin[chunk_expert]) * ROW_CHUNK,
                                        ROW_CHUNK), 0).astype(I32)
    idx_k = idx[:, :TOP_K]
    dest = group_start[idx_k].astype(I32) + rank[:, :TOP_K]
    dest3 = dest.reshape(n // TOK_TILE, 1, TOK_TILE * TOP_K)
    zstart = ((group_start + counts) // SUB_ROWS * SUB_ROWS).astype(I32)
    rows_total = (n_chunks + 1) * ROW_CHUNK

    xs = _dispatch(zstart, dest3, x1, rows_total=rows_total, n_experts=n_experts)
    ys = _experts(chunk_expert, chunk_valid, total.reshape(1).astype(I32), xs, w1, w3, w2,
                  n_chunks=n_chunks)
    ysh = _shared_expert(x1b, w13s, w_shared2.astype(BF16))
    out = _combine(dest3, x1, ysh, gate, ln2_g.astype(F32).reshape(1, d),
                   ln2_b.astype(F32).reshape(1, d), ys)
    return out.reshape(batch, seq, d)


def kernel(x, w_in, b_forget, pool_w, pool_scale, w_branch_pool, w_branch_attn, w_out, ln1_g, ln1_b,
           w_router, router_bias, w1, w3, w2, w_shared1, w_shared3, w_shared2, ln2_g, ln2_b):
    for l in range(w_in.shape[0]):
        x = _layer(x, w_in[l], b_forget[l], pool_w[l], pool_scale[l], w_branch_pool[l],
                   w_branch_attn[l], w_out[l], ln1_g[l], ln1_b[l], w_router[l], router_bias[l],
                   w1[l], w3[l], w2[l], w_shared1[l], w_shared3[l], w_shared2[l], ln2_g[l], ln2_b[l])
    return x
```

```python
import functools

import jax
import jax.numpy as jnp
from jax import lax
from jax.experimental import pallas as pl
from jax.experimental.pallas import tpu as pltpu

F32 = jnp.float32
BF16 = jnp.bfloat16
I32 = jnp.int32

POOL_WINDOWS = (2, 4, 8, 16)
HEAD_DIM = 128
TOP_K = 6
ROUTE_SCALE = 1.0
DEPTH = 1
ALPHA = (2 * DEPTH) ** 0.25
LN_EPS = 1e-5

LANES = 128
MAX_WINDOW = max(POOL_WINDOWS)
ATTN_BLOCK = 512
Q_HALVES = 2
ROW_CHUNK = 1024
SUB_ROWS = 256
UP_K_TILE = 512
DOWN_N_TILE = 512
SHARED_F_TILE = 256
TOK_TILE = 256
COMBINE_TILE = 128
VMEM_LIMIT = 56 * 1024 * 1024


def _cparams(sem, vmem=VMEM_LIMIT):
    return pltpu.CompilerParams(dimension_semantics=sem, vmem_limit_bytes=vmem)


def _split3(v):
    hi = v.astype(BF16)
    r1 = v - hi.astype(F32)
    mid = r1.astype(BF16)
    lo = (r1 - mid.astype(F32)).astype(BF16)
    return hi, mid, lo


def _layer_norm(v, g, b):
    mu = jnp.mean(v, axis=-1, keepdims=True)
    d = v - mu
    var = jnp.mean(d * d, axis=-1, keepdims=True)
    return d * lax.rsqrt(var + LN_EPS) * g + b


def _inproj_kernel(x_ref, w_ref, o_ref, xb_ref, *, bn, q_lo, q_hi, q_scale):
    j = pl.program_id(1)

    @pl.when(j == 0)
    def _():
        xb_ref[...] = x_ref[...].astype(BF16)

    acc = jnp.dot(xb_ref[...], w_ref[...], preferred_element_type=F32)
    col0 = j * bn
    s = jnp.where((col0 >= q_lo) & (col0 < q_hi), q_scale, 1.0).astype(F32)
    o_ref[...] = (acc * s).astype(BF16)


def _inproj(x2, w_main, *, q_lo, q_hi, q_scale, bm=1024, bn=512):
    n, d = x2.shape
    c = w_main.shape[1]
    return pl.pallas_call(
        functools.partial(_inproj_kernel, bn=bn, q_lo=q_lo, q_hi=q_hi, q_scale=q_scale),
        grid=(n // bm, c // bn),
        in_specs=[pl.BlockSpec((bm, d), lambda i, j: (i, 0)),
                  pl.BlockSpec((d, bn), lambda i, j: (0, j))],
        out_specs=pl.BlockSpec((bm, bn), lambda i, j: (i, j)),
        out_shape=jax.ShapeDtypeStruct((n, c), BF16),
        scratch_shapes=[pltpu.VMEM((bm, d), BF16)],
        compiler_params=_cparams(("parallel", "arbitrary")),
        name="inproj",
    )(x2, w_main)


def _forget_kernel(x_ref, wf_ref, b_ref, sel_ref, ka_ref, carry_ref, *, ts, n_heads):
    i = pl.program_id(1)

    @pl.when(i == 0)
    def _():
        carry_ref[...] = jnp.zeros_like(carry_ref)

    z = jnp.dot(x_ref[...].astype(BF16), wf_ref[...], preferred_element_type=F32) + b_ref[...]
    lf = jnp.minimum(z, 0.0) - jnp.log1p(jnp.exp(-jnp.abs(z)))
    lane = lax.broadcasted_iota(I32, lf.shape, 1)
    lf = jnp.where(lane < n_heads, lf, 0.0)

    row = lax.broadcasted_iota(I32, (ts, ts), 0)
    col = lax.broadcasted_iota(I32, (ts, ts), 1)
    tri = jnp.where(col <= row, 1.0, 0.0).astype(BF16)
    hi, mid, lo = _split3(lf)
    cs = (jnp.dot(tri, hi, preferred_element_type=F32)
          + jnp.dot(tri, mid, preferred_element_type=F32)
          + jnp.dot(tri, lo, preferred_element_type=F32)) + carry_ref[...]
    carry_ref[...] = cs[ts - 1:ts, :]

    h2, m2, l2 = _split3(-cs)
    ka = (jnp.dot(h2, sel_ref[0], preferred_element_type=F32)
          + jnp.dot(m2, sel_ref[1], preferred_element_type=F32)
          + jnp.dot(l2, sel_ref[2], preferred_element_type=F32))
    ka_ref[...] = ka.astype(BF16)


def _forget_bias(x2, wf_pad, b_pad, sel, *, batch, seq, n_heads, ts=512):
    n, d = x2.shape
    nt = seq // ts
    return pl.pallas_call(
        functools.partial(_forget_kernel, ts=ts, n_heads=n_heads),
        grid=(batch, nt),
        in_specs=[pl.BlockSpec((ts, d), lambda b, i: (b * nt + i, 0)),
                  pl.BlockSpec((d, LANES), lambda b, i: (0, 0)),
                  pl.BlockSpec((1, LANES), lambda b, i: (0, 0)),
                  pl.BlockSpec((3, LANES, n_heads * LANES), lambda b, i: (0, 0, 0))],
        out_specs=pl.BlockSpec((ts, n_heads * LANES), lambda b, i: (b * nt + i, 0)),
        out_shape=jax.ShapeDtypeStruct((n, n_heads * LANES), BF16),
        scratch_shapes=[pltpu.VMEM((1, LANES), F32)],
        compiler_params=_cparams(("arbitrary", "arbitrary")),
        name="forget_bias",
    )(x2, wf_pad, b_pad, sel)


def _pool_kernel(u_ref, halo_ref, pw_ref, sc_ref, o_ref, buf_ref, *, ts, gdim):
    i = pl.program_id(1)
    h = MAX_WINDOW
    halo = halo_ref[...].astype(F32)
    buf_ref[0:h, :] = jnp.where(i > 0, halo, 0.0)
    buf_ref[h:h + ts, :] = u_ref[...].astype(F32)

    t = i * ts + lax.broadcasted_iota(I32, (ts, 1), 0)
    for g, w in enumerate(POOL_WINDOWS):
        cs = slice(g * gdim, (g + 1) * gdim)
        tok = buf_ref[h:h + ts, cs]
        wsum = tok
        for k in range(1, w):
            wsum = wsum + buf_ref[h - k:h - k + ts, cs]
        cnt = jnp.minimum(t + 1, w).astype(F32)
        p = (wsum / cnt - tok).astype(BF16)
        y = jnp.dot(p, pw_ref[g], preferred_element_type=F32) * sc_ref[:, cs]
        o_ref[:, cs] = y.astype(BF16)


def _pool(z, pool_w_b, pool_scale2, *, batch, seq, width, ts=512):
    n = z.shape[0]
    groups = len(POOL_WINDOWS)
    gdim = width // groups
    nt = seq // ts
    hb = ts // MAX_WINDOW
    return pl.pallas_call(
        functools.partial(_pool_kernel, ts=ts, gdim=gdim),
        grid=(batch, nt),
        in_specs=[pl.BlockSpec((ts, width), lambda b, i: (b * nt + i, 0)),
                  pl.BlockSpec((MAX_WINDOW, width),
                               lambda b, i: (jnp.maximum((b * nt + i) * hb - 1, 0), 0)),
                  pl.BlockSpec((groups, gdim, gdim), lambda b, i: (0, 0, 0)),
                  pl.BlockSpec((1, width), lambda b, i: (0, 0))],
        out_specs=pl.BlockSpec((ts, width), lambda b, i: (b * nt + i, 0)),
        out_shape=jax.ShapeDtypeStruct((n, width), BF16),
        scratch_shapes=[pltpu.VMEM((ts + MAX_WINDOW, width), F32)],
        compiler_params=_cparams(("parallel", "parallel")),
        name="pool_mixer",
    )(z, z, pool_w_b, pool_scale2)


def _attn_kernel(q_ref, k_ref, ka_ref, v_ref, o_ref, qa_ref, m_ref, l_ref, acc_ref, *, blk):
    qi = pl.program_id(2)
    lane = lax.broadcasted_iota(I32, (blk, HEAD_DIM), 1)
    ones3 = jnp.where(lane < 3, 1.0, 0.0).astype(BF16)
    for h in range(Q_HALVES):
        qa_ref[h] = jnp.concatenate([q_ref[h * blk:(h + 1) * blk, :], ones3], axis=1)
    m_ref[...] = jnp.full_like(m_ref, -jnp.inf)
    l_ref[...] = jnp.zeros_like(l_ref)
    acc_ref[...] = jnp.zeros_like(acc_ref)
    reps = blk // LANES

    def step(h, ki, masked):
        ks = pl.multiple_of(ki * blk, blk)
        k_aug = jnp.concatenate([k_ref[pl.ds(ks, blk), :], ka_ref[pl.ds(ks, blk), :]], axis=1)
        s = lax.dot_general(qa_ref[h], k_aug, (((1,), (1,)), ((), ())),
                            preferred_element_type=F32)
        if masked:
            r = lax.broadcasted_iota(I32, (blk, blk), 0)
            c = lax.broadcasted_iota(I32, (blk, blk), 1)
            s = jnp.where(c <= r, s, -jnp.inf)
        m_prev = m_ref[h]
        m_next = jnp.maximum(m_prev, jnp.max(s, axis=1, keepdims=True))
        alpha = jnp.exp(m_prev - m_next)
        p = jnp.exp(s - jnp.concatenate([m_next] * reps, axis=1))
        l_ref[h] = alpha * l_ref[h] + jnp.sum(p, axis=1, keepdims=True)
        acc_ref[h] = alpha * acc_ref[h] + jnp.dot(
            p.astype(BF16), v_ref[pl.ds(ks, blk), :], preferred_element_type=F32)
        m_ref[h] = m_next

    def body(kq, carry):
        for d in range(Q_HALVES):
            for h in range(Q_HALVES):
                step(h, Q_HALVES * kq + d, False)
        return carry

    lax.fori_loop(0, qi, body, 0)
    for h in range(Q_HALVES):
        for d in range(h):
            step(h, Q_HALVES * qi + d, False)
        step(h, Q_HALVES * qi + h, True)
        o_ref[h * blk:(h + 1) * blk, :] = (acc_ref[h] / l_ref[h]).astype(BF16)


def _attention(z, ka, *, batch, seq, n_heads, q_col, k_col, v_col, blk=ATTN_BLOCK):
    n = z.shape[0]
    qb = Q_HALVES * blk
    nq = seq // qb
    return pl.pallas_call(
        functools.partial(_attn_kernel, blk=blk),
        grid=(batch, n_heads, nq),
        in_specs=[pl.BlockSpec((qb, HEAD_DIM), lambda b, h, i: (b * nq + i, q_col + h)),
                  pl.BlockSpec((seq, HEAD_DIM), lambda b, h, i: (b, k_col + h)),
                  pl.BlockSpec((seq, HEAD_DIM), lambda b, h, i: (b, h)),
                  pl.BlockSpec((seq, HEAD_DIM), lambda b, h, i: (b, v_col + h))],
        out_specs=pl.BlockSpec((qb, HEAD_DIM), lambda b, h, i: (b * nq + i, h)),
        out_shape=jax.ShapeDtypeStruct((n, n_heads * HEAD_DIM), BF16),
        scratch_shapes=[pltpu.VMEM((Q_HALVES, blk, 2 * HEAD_DIM), BF16),
                        pltpu.VMEM((Q_HALVES, blk, LANES), F32),
                        pltpu.VMEM((Q_HALVES, blk, LANES), F32),
                        pltpu.VMEM((Q_HALVES, blk, HEAD_DIM), F32)],
        compiler_params=_cparams(("parallel", "parallel", "arbitrary")),
        name="fox_attention",
    )(z, z, ka, z)


def _merge_kernel(x_ref, p_ref, a_ref, wg0_ref, wg1_ref, wbp_ref, wba_ref, o_ref, xb_ref):
    j = pl.program_id(1)

    @pl.when(j == 0)
    def _():
        xb_ref[...] = x_ref[...].astype(BF16)

    xb = xb_ref[...]
    g0 = jax.nn.sigmoid(jnp.dot(xb, wg0_ref[...], preferred_element_type=F32))
    g1 = jax.nn.sigmoid(jnp.dot(xb, wg1_ref[...], preferred_element_type=F32))
    yp = jnp.dot(p_ref[...], wbp_ref[...], preferred_element_type=F32)
    ya = jnp.dot(a_ref[...], wba_ref[...], preferred_element_type=F32)
    o_ref[...] = (g0 * yp + g1 * ya).astype(BF16)


def _merge(x2, pool_o, attn_o, wg, wbp, wba, *, bm=1024, bn=512):
    n, d = x2.shape
    wp = pool_o.shape[1]
    wa = attn_o.shape[1]
    nj = d // bn
    return pl.pallas_call(
        _merge_kernel,
        grid=(n // bm, nj),
        in_specs=[pl.BlockSpec((bm, d), lambda i, j: (i, 0)),
                  pl.BlockSpec((bm, wp), lambda i, j: (i, 0)),
                  pl.BlockSpec((bm, wa), lambda i, j: (i, 0)),
                  pl.BlockSpec((d, bn), lambda i, j: (0, j)),
                  pl.BlockSpec((d, bn), lambda i, j: (0, nj + j)),
                  pl.BlockSpec((wp, bn), lambda i, j: (0, j)),
                  pl.BlockSpec((wa, bn), lambda i, j: (0, j))],
        out_specs=pl.BlockSpec((bm, bn), lambda i, j: (i, j)),
        out_shape=jax.ShapeDtypeStruct((n, d), BF16),
        scratch_shapes=[pltpu.VMEM((bm, d), BF16)],
        compiler_params=_cparams(("parallel", "arbitrary")),
        name="branch_merge",
    )(x2, pool_o, attn_o, wg, wg, wbp, wba)


def _out_ln_router_kernel(x_ref, m_ref, wo_ref, g_ref, b_ref, wrh_ref, wrl_ref,
                          x1_ref, x1b_ref, lg_ref):
    mix = jnp.dot(m_ref[...], wo_ref[...], preferred_element_type=F32)
    x1 = _layer_norm(ALPHA * x_ref[...] + mix, g_ref[...], b_ref[...])
    x1_ref[...] = x1
    hi = x1.astype(BF16)
    x1b_ref[...] = hi
    lo = (x1 - hi.astype(F32)).astype(BF16)
    lg_ref[...] = (jnp.dot(hi, wrh_ref[...], preferred_element_type=F32)
                   + jnp.dot(lo, wrh_ref[...], preferred_element_type=F32)
                   + jnp.dot(hi, wrl_ref[...], preferred_element_type=F32))


def _out_ln_router(x2, merged, wo, g1, b1, wr_hi, wr_lo, *, bm=512):
    n, d = x2.shape
    row = lambda i: (i, 0)
    fix = lambda i: (0, 0)
    return pl.pallas_call(
        _out_ln_router_kernel,
        grid=(n // bm,),
        in_specs=[pl.BlockSpec((bm, d), row), pl.BlockSpec((bm, d), row),
                  pl.BlockSpec((d, d), fix), pl.BlockSpec((1, d), fix), pl.BlockSpec((1, d), fix),
                  pl.BlockSpec((d, LANES), fix), pl.BlockSpec((d, LANES), fix)],
        out_specs=[pl.BlockSpec((bm, d), row), pl.BlockSpec((bm, d), row),
                   pl.BlockSpec((bm, LANES), row)],
        out_shape=[jax.ShapeDtypeStruct((n, d), F32), jax.ShapeDtypeStruct((n, d), BF16),
                   jax.ShapeDtypeStruct((n, LANES), F32)],
        compiler_params=_cparams(("parallel",)),
        name="outproj_ln1_router",
    )(x2, merged, wo, g1, b1, wr_hi, wr_lo)


def _route_kernel(lg_ref, bias_ref, idx_ref, gate_ref, rank_ref, cnt_ref, carry_ref,
                  *, bm, n_experts):
    i = pl.program_id(0)

    @pl.when(i == 0)
    def _():
        carry_ref[...] = jnp.zeros_like(carry_ref)

    lane = lax.broadcasted_iota(I32, (bm, LANES), 1)
    lane_f = lane.astype(F32)
    scores = jax.nn.sigmoid(lg_ref[...])
    sel = jnp.where(lane < n_experts, scores + bias_ref[...], -jnp.inf)

    hits, gates = [], []
    gsum = jnp.zeros((bm, 1), F32)
    member = jnp.zeros((bm, LANES), F32)
    idx_out = jnp.zeros((bm, LANES), F32)
    for k in range(TOP_K):
        m = jnp.max(sel, axis=1, keepdims=True)
        ik = jnp.min(jnp.where(sel == m, lane_f, float(LANES)), axis=1, keepdims=True)
        hit = lane_f == ik
        gk = jnp.sum(jnp.where(hit, scores, 0.0), axis=1, keepdims=True)
        sel = jnp.where(hit, -jnp.inf, sel)
        member = jnp.where(hit, 1.0, member)
        idx_out = jnp.where(lane == k, ik, idx_out)
        gsum = gsum + gk
        hits.append(hit)
        gates.append(gk)

    row = lax.broadcasted_iota(I32, (bm, bm), 0)
    col = lax.broadcasted_iota(I32, (bm, bm), 1)
    before = jnp.where(col < row, 1.0, 0.0).astype(BF16)
    prefix = jnp.dot(before, member.astype(BF16), preferred_element_type=F32) + carry_ref[...]

    gate_out = jnp.zeros((bm, LANES), F32)
    rank_out = jnp.zeros((bm, LANES), F32)
    for k in range(TOP_K):
        rk = jnp.sum(jnp.where(hits[k], prefix, 0.0), axis=1, keepdims=True)
        rank_out = jnp.where(lane == k, rk, rank_out)
        gate_out = jnp.where(lane == k, gates[k] / gsum * ROUTE_SCALE, gate_out)

    total = carry_ref[...] + jnp.sum(member, axis=0, keepdims=True)
    carry_ref[...] = total
    idx_ref[...] = idx_out.astype(I32)
    gate_ref[...] = gate_out
    rank_ref[...] = rank_out.astype(I32)
    cnt_ref[...] = jnp.broadcast_to(total, cnt_ref.shape).astype(I32)


def _route(logits, bias_pad, *, n_experts, bm=TOK_TILE):
    n = logits.shape[0]
    row = lambda i: (i, 0)
    fix = lambda i: (0, 0)
    return pl.pallas_call(
        functools.partial(_route_kernel, bm=bm, n_experts=n_experts),
        grid=(n // bm,),
        in_specs=[pl.BlockSpec((bm, LANES), row), pl.BlockSpec((1, LANES), fix)],
        out_specs=[pl.BlockSpec((bm, LANES), row), pl.BlockSpec((bm, LANES), row),
                   pl.BlockSpec((bm, LANES), row), pl.BlockSpec((8, LANES), fix)],
        out_shape=[jax.ShapeDtypeStruct((n, LANES), I32), jax.ShapeDtypeStruct((n, LANES), F32),
                   jax.ShapeDtypeStruct((n, LANES), I32), jax.ShapeDtypeStruct((8, LANES), I32)],
        scratch_shapes=[pltpu.VMEM((1, LANES), F32)],
        compiler_params=_cparams(("arbitrary",)),
        name="route_topk",
    )(logits, bias_pad)


def _dispatch_kernel(zs_ref, dest_ref, x_ref, xs_hbm, zero_ref, sem, zsem, *, bt, n_experts):
    i = pl.program_id(0)

    def zero_copy(e):
        zs = pl.multiple_of(zs_ref[e], SUB_ROWS)
        return pltpu.make_async_copy(zero_ref, xs_hbm.at[pl.ds(zs, SUB_ROWS), :], zsem)

    @pl.when(i == 0)
    def _():
        zero_ref[...] = jnp.zeros_like(zero_ref)

        def zstart(e, c):
            zero_copy(e).start()
            return c

        def zwait(e, c):
            zero_copy(e).wait()
            return c

        lax.fori_loop(0, n_experts, zstart, 0)
        lax.fori_loop(0, n_experts, zwait, 0)

    def row_copy(t, k):
        dst = dest_ref[0, 0, t * TOP_K + k]
        return pltpu.make_async_copy(x_ref.at[pl.ds(t, 1), :], xs_hbm.at[pl.ds(dst, 1), :], sem)

    def start(t, c):
        for k in range(TOP_K):
            row_copy(t, k).start()
        return c

    def wait(t, c):
        for k in range(TOP_K):
            row_copy(t, k).wait()
        return c

    lax.fori_loop(0, bt, start, 0)
    lax.fori_loop(0, bt, wait, 0)


def _dispatch(zstart, dest3, x1, *, rows_total, n_experts, bt=TOK_TILE):
    n, d = x1.shape
    grid_spec = pltpu.PrefetchScalarGridSpec(
        num_scalar_prefetch=1,
        grid=(n // bt,),
        in_specs=[pl.BlockSpec((1, 1, bt * TOP_K), lambda i, zs: (i, 0, 0),
                               memory_space=pltpu.SMEM),
                  pl.BlockSpec((bt, d), lambda i, zs: (i, 0))],
        out_specs=pl.BlockSpec(memory_space=pl.ANY),
        scratch_shapes=[pltpu.VMEM((SUB_ROWS, d), F32),
                        pltpu.SemaphoreType.DMA(()),
                        pltpu.SemaphoreType.DMA(())],
    )
    return pl.pallas_call(
        functools.partial(_dispatch_kernel, bt=bt, n_experts=n_experts),
        grid_spec=grid_spec,
        out_shape=jax.ShapeDtypeStruct((rows_total, d), F32),
        compiler_params=_cparams(("arbitrary",)),
        name="dispatch_rows",
    )(zstart, dest3, x1)


def _experts_kernel(ce_ref, cv_ref, tot_ref, x_ref, w1_ref, w3_ref, w2_ref, o_ref,
                    w13_ref, acc_ref, h_ref, w2b_ref, *, n_up, fdim):
    c = pl.program_id(0)
    j = pl.program_id(1)
    nv = cv_ref[c]
    del ce_ref, tot_ref
    all_rows = slice(0, ROW_CHUNK)
    sub_blocks = [slice(sb * SUB_ROWS, (sb + 1) * SUB_ROWS) for sb in range(ROW_CHUNK // SUB_ROWS)]

    def up_rows(rows, first, last):
        part = jnp.dot(x_ref[rows, :].astype(BF16), w13_ref[...], preferred_element_type=F32)
        if not first:
            part = part + acc_ref[rows, :]
        if last:
            h1 = part[:, :fdim]
            h_ref[rows, :] = (h1 * jax.nn.sigmoid(h1) * part[:, fdim:]).astype(BF16)
        else:
            acc_ref[rows, :] = part

    def cast_up_weights():
        w13_ref[:, :fdim] = w1_ref[...].astype(BF16)
        w13_ref[:, fdim:] = w3_ref[...].astype(BF16)

    def up(first, last):
        @pl.when(nv == ROW_CHUNK)
        def _():
            cast_up_weights()
            up_rows(all_rows, first, last)

        @pl.when(nv < ROW_CHUNK)
        def _():
            cast_up_weights()
            for rows in sub_blocks:
                @pl.when(rows.start < nv)
                def _():
                    up_rows(rows, first, last)

    @pl.when(j == 0)
    def _():
        up(True, n_up == 1)

    if n_up > 2:
        @pl.when((j > 0) & (j < n_up - 1))
        def _():
            up(False, False)

    if n_up > 1:
        @pl.when(j == n_up - 1)
        def _():
            up(False, True)

    @pl.when(j >= n_up)
    def _():
        @pl.when(nv == ROW_CHUNK)
        def _():
            w2b_ref[...] = w2_ref[...].astype(BF16)
            o_ref[...] = jnp.dot(h_ref[...], w2b_ref[...], preferred_element_type=F32)

        @pl.when(nv < ROW_CHUNK)
        def _():
            w2b_ref[...] = w2_ref[...].astype(BF16)
            for rows in sub_blocks:
                @pl.when(rows.start < nv)
                def _():
                    o_ref[rows, :] = jnp.dot(h_ref[rows, :], w2b_ref[...],
                                             preferred_element_type=F32)

                @pl.when(rows.start >= nv)
                def _():
                    o_ref[rows, :] = jnp.zeros((SUB_ROWS, o_ref.shape[1]), F32)


def _experts(chunk_expert, chunk_valid, total, xs, w1, w3, w2, *, n_chunks):
    d = xs.shape[1]
    fdim = w1.shape[2]
    n_up = d // UP_K_TILE
    n_down = d // DOWN_N_TILE

    def x_map(c, j, ce, cv, tot):
        return (jnp.minimum(c, tot[0] - 1), jnp.minimum(j, n_up - 1))

    def w13_map(c, j, ce, cv, tot):
        return (ce[c], jnp.minimum(j, n_up - 1), 0)

    def w2_map(c, j, ce, cv, tot):
        return (ce[c], 0, jnp.maximum(j - n_up, 0))

    def o_map(c, j, ce, cv, tot):
        return (c, jnp.maximum(j - n_up, 0))

    grid_spec = pltpu.PrefetchScalarGridSpec(
        num_scalar_prefetch=3,
        grid=(n_chunks, n_up + n_down),
        in_specs=[pl.BlockSpec((ROW_CHUNK, UP_K_TILE), x_map),
                  pl.BlockSpec((None, UP_K_TILE, fdim), w13_map),
                  pl.BlockSpec((None, UP_K_TILE, fdim), w13_map),
                  pl.BlockSpec((None, fdim, DOWN_N_TILE), w2_map)],
        out_specs=pl.BlockSpec((ROW_CHUNK, DOWN_N_TILE), o_map),
        scratch_shapes=[pltpu.VMEM((UP_K_TILE, 2 * fdim), BF16),
                        pltpu.VMEM((ROW_CHUNK, 2 * fdim), F32),
                        pltpu.VMEM((ROW_CHUNK, fdim), BF16),
                        pltpu.VMEM((fdim, DOWN_N_TILE), BF16)],
    )
    return pl.pallas_call(
        functools.partial(_experts_kernel, n_up=n_up, fdim=fdim),
        grid_spec=grid_spec,
        out_shape=jax.ShapeDtypeStruct((n_chunks * ROW_CHUNK, d), F32),
        compiler_params=_cparams(("arbitrary", "arbitrary")),
        name="routed_experts",
    )(chunk_expert, chunk_valid, total, xs, w1, w3, w2)


def _shared_kernel(x_ref, w13_ref, w2_ref, o_ref, *, tf):
    f = pl.program_id(1)
    r = jnp.dot(x_ref[...], w13_ref[...], preferred_element_type=F32)
    h1 = r[:, :tf]
    h = (h1 * jax.nn.sigmoid(h1) * r[:, tf:]).astype(BF16)
    y = jnp.dot(h, w2_ref[...], preferred_element_type=F32)

    @pl.when(f == 0)
    def _():
        o_ref[...] = y

    @pl.when(f > 0)
    def _():
        o_ref[...] += y


def _shared_expert(x1b, w13s, w2s, *, bm=1024, tf=SHARED_F_TILE):
    n, d = x1b.shape
    nf = w2s.shape[0] // tf
    return pl.pallas_call(
        functools.partial(_shared_kernel, tf=tf),
        grid=(n // bm, nf),
        in_specs=[pl.BlockSpec((bm, d), lambda i, f: (i, 0)),
                  pl.BlockSpec((d, 2 * tf), lambda i, f: (0, f)),
                  pl.BlockSpec((tf, d), lambda i, f: (f, 0))],
        out_specs=pl.BlockSpec((bm, d), lambda i, f: (i, 0)),
        out_shape=jax.ShapeDtypeStruct((n, d), F32),
        compiler_params=_cparams(("parallel", "arbitrary")),
        name="shared_expert",
    )(x1b, w13s, w2s)


def _combine_kernel(dcur_ref, dnext_ref, x1_ref, ysh_ref, gate_ref, g_ref, b_ref, ys_hbm, o_ref,
                    buf_ref, sem, *, bt):
    i = pl.program_id(0)
    slot = lax.rem(i, 2)

    def row_copy(dref, s, t, k):
        src = dref[0, 0, t * TOP_K + k]
        return pltpu.make_async_copy(ys_hbm.at[pl.ds(src, 1), :],
                                     buf_ref.at[s, k, pl.ds(t, 1), :], sem.at[s])

    def issue(dref, s):
        def start(t, c):
            for k in range(TOP_K):
                row_copy(dref, s, t, k).start()
            return c

        lax.fori_loop(0, bt, start, 0)

    @pl.when(i == 0)
    def _():
        issue(dcur_ref, slot)

    @pl.when(i + 1 < pl.num_programs(0))
    def _():
        issue(dnext_ref, 1 - slot)

    acc = ALPHA * x1_ref[...] + ysh_ref[...]

    def wait(t, c):
        for k in range(TOP_K):
            row_copy(dcur_ref, slot, t, k).wait()
        return c

    lax.fori_loop(0, bt, wait, 0)
    gate = gate_ref[...]
    for k in range(TOP_K):
        acc = acc + gate[:, k:k + 1] * buf_ref[slot, k]
    o_ref[...] = _layer_norm(acc, g_ref[...], b_ref[...])


def _combine(dest3, x1, ysh, gate, g2, b2, ys, *, bt=COMBINE_TILE):
    n, d = x1.shape
    nt = n // bt
    row = lambda i: (i, 0)
    fix = lambda i: (0, 0)
    return pl.pallas_call(
        functools.partial(_combine_kernel, bt=bt),
        grid=(nt,),
        in_specs=[pl.BlockSpec((1, 1, bt * TOP_K), lambda i: (i, 0, 0), memory_space=pltpu.SMEM),
                  pl.BlockSpec((1, 1, bt * TOP_K), lambda i: (jnp.minimum(i + 1, nt - 1), 0, 0),
                               memory_space=pltpu.SMEM),
                  pl.BlockSpec((bt, d), row), pl.BlockSpec((bt, d), row),
                  pl.BlockSpec((bt, LANES), row),
                  pl.BlockSpec((1, d), fix), pl.BlockSpec((1, d), fix),
                  pl.BlockSpec(memory_space=pl.ANY)],
        out_specs=pl.BlockSpec((bt, d), row),
        out_shape=jax.ShapeDtypeStruct((n, d), F32),
        scratch_shapes=[pltpu.VMEM((2, TOP_K, bt, d), F32), pltpu.SemaphoreType.DMA((2,))],
        compiler_params=_cparams(("arbitrary",)),
        name="combine_ln2",
    )(dest3, dest3, x1, ysh, gate, g2, b2, ys)


def _layer(x, w_in, b_forget, pool_w, pool_scale, w_branch_pool, w_branch_attn, w_out,
           ln1_g, ln1_b, w_router, router_bias, w1, w3, w2, w_shared1, w_shared3, w_shared2,
           ln2_g, ln2_b):
    batch, seq, d = x.shape
    n = batch * seq
    pool_width = w_branch_pool.shape[0]
    attn_width = w_branch_attn.shape[0]
    n_heads = attn_width // HEAD_DIM
    n_experts = w1.shape[0]
    off_q = pool_width
    off_k = off_q + attn_width
    off_v = off_k + attn_width
    off_f = off_v + attn_width
    off_gate = off_f + n_heads

    x2 = x.reshape(n, d)

    w_main = w_in[:, :off_f].astype(BF16)
    wf_pad = jnp.pad(w_in[:, off_f:off_gate], ((0, 0), (0, LANES - n_heads))).astype(BF16)
    bf_pad = jnp.pad(b_forget.astype(F32), (0, LANES - n_heads)).reshape(1, LANES)
    w_gate = w_in[:, off_gate:].astype(BF16)
    piece = jnp.arange(3)[:, None, None]
    src = jnp.arange(LANES)[None, :, None]
    dst = jnp.arange(n_heads * LANES)[None, None, :]
    sel = ((src < n_heads) & (dst == src * LANES + piece)).astype(BF16)
    wr = jnp.pad(w_router.astype(F32), ((0, 0), (0, LANES - n_experts)))
    wr_hi = wr.astype(BF16)
    wr_lo = (wr - wr_hi.astype(F32)).astype(BF16)
    rb_pad = jnp.pad(router_bias.astype(F32), (0, LANES - n_experts)).reshape(1, LANES)
    tf = SHARED_F_TILE
    fs = w_shared1.shape[1]
    w13s = jnp.concatenate([w_shared1.reshape(d, fs // tf, tf), w_shared3.reshape(d, fs // tf, tf)],
                           axis=2).reshape(d, 2 * fs).astype(BF16)

    z = _inproj(x2, w_main, q_lo=off_q, q_hi=off_k, q_scale=HEAD_DIM ** -0.5)
    ka = _forget_bias(x2, wf_pad, bf_pad, sel, batch=batch, seq=seq, n_heads=n_heads)
    pool_o = _pool(z, pool_w.astype(BF16), pool_scale.astype(F32).reshape(1, pool_width),
                   batch=batch, seq=seq, width=pool_width)
    attn_o = _attention(z, ka, batch=batch, seq=seq, n_heads=n_heads,
                        q_col=off_q // HEAD_DIM, k_col=off_k // HEAD_DIM, v_col=off_v // HEAD_DIM)
    merged = _merge(x2, pool_o, attn_o, w_gate, w_branch_pool.astype(BF16),
                    w_branch_attn.astype(BF16))
    x1, x1b, logits = _out_ln_router(x2, merged, w_out.astype(BF16),
                                     ln1_g.astype(F32).reshape(1, d), ln1_b.astype(F32).reshape(1, d),
                                     wr_hi, wr_lo)

    idx, gate, rank, cnt = _route(logits, rb_pad, n_experts=n_experts)
    counts = cnt[0, :n_experts]
    n_assign = n * TOP_K
    n_chunks = -(-n_assign // ROW_CHUNK) + n_experts
    chunks_per = (counts + ROW_CHUNK - 1) // ROW_CHUNK
    chunk_end = jnp.cumsum(chunks_per)
    chunk_begin = chunk_end - chunks_per
    total = chunk_end[-1]
    group_start = chunk_begin * ROW_CHUNK
    cids = jnp.arange(n_chunks, dtype=I32)
    c_eff = jnp.minimum(cids, total - 1)
    chunk_expert = jnp.minimum(jnp.sum((chunk_end[None, :] <= c_eff[:, None]).astype(I32), axis=1),
                               n_experts - 1).astype(I32)
    chunk_valid = jnp.where(cids < total,
                            jnp.minimum(counts[chunk_expert] - (cids - chunk_begin[chunk_expert]) * ROW_CHUNK,
                                        ROW_CHUNK), 0).astype(I32)
    idx_k = idx[:, :TOP_K]
    dest = group_start[idx_k].astype(I32) + rank[:, :TOP_K]
    dest3 = dest.reshape(n // TOK_TILE, 1, TOK_TILE * TOP_K)
    zstart = ((group_start + counts) // SUB_ROWS * SUB_ROWS).astype(I32)
    rows_total = (n_chunks + 1) * ROW_CHUNK

    xs = _dispatch(zstart, dest3, x1, rows_total=rows_total, n_experts=n_experts)
    ys = _experts(chunk_expert, chunk_valid, total.reshape(1).astype(I32), xs, w1, w3, w2,
                  n_chunks=n_chunks)
    ysh = _shared_expert(x1b, w13s, w_shared2.astype(BF16))
    dest_c = dest.reshape(n // COMBINE_TILE, 1, COMBINE_TILE * TOP_K)
    out = _combine(dest_c, x1, ysh, gate, ln2_g.astype(F32).reshape(1, d),
                   ln2_b.astype(F32).reshape(1, d), ys)
    return out.reshape(batch, seq, d)


def kernel(x, w_in, b_forget, pool_w, pool_scale, w_branch_pool, w_branch_attn, w_out, ln1_g, ln1_b,
           w_router, router_bias, w1, w3, w2, w_shared1, w_shared3, w_shared2, ln2_g, ln2_b):
    for l in range(w_in.shape[0]):
        x = _layer(x, w_in[l], b_forget[l], pool_w[l], pool_scale[l], w_branch_pool[l],
                   w_branch_attn[l], w_out[l], ln1_g[l], ln1_b[l], w_router[l], router_bias[l],
                   w1[l], w3[l], w2[l], w_shared1[l], w_shared3[l], w_shared2[l], ln2_g[l], ln2_b[l])
    return x
```

```python
import functools

import jax
import jax.numpy as jnp
from jax import lax
from jax.experimental import pallas as pl
from jax.experimental.pallas import tpu as pltpu

F32 = jnp.float32
BF16 = jnp.bfloat16
I32 = jnp.int32

POOL_WINDOWS = (2, 4, 8, 16)
HEAD_DIM = 128
TOP_K = 6
ROUTE_SCALE = 1.0
DEPTH = 1
ALPHA = (2 * DEPTH) ** 0.25
LN_EPS = 1e-5

LANES = 128
MAX_WINDOW = max(POOL_WINDOWS)
ATTN_BLOCK = 512
Q_HALVES = 2
SKIP_MARGIN = 110.0
ROW_CHUNK = 1024
SUB_ROWS = 256
UP_K_TILE = 512
DOWN_N_TILE = 512
SHARED_F_TILE = 256
TOK_TILE = 256
COMBINE_TILE = 128
VMEM_LIMIT = 56 * 1024 * 1024


def _cparams(sem, vmem=VMEM_LIMIT):
    return pltpu.CompilerParams(dimension_semantics=sem, vmem_limit_bytes=vmem)


def _split3(v):
    hi = v.astype(BF16)
    r1 = v - hi.astype(F32)
    mid = r1.astype(BF16)
    lo = (r1 - mid.astype(F32)).astype(BF16)
    return hi, mid, lo


def _layer_norm(v, g, b):
    mu = jnp.mean(v, axis=-1, keepdims=True)
    d = v - mu
    var = jnp.mean(d * d, axis=-1, keepdims=True)
    return d * lax.rsqrt(var + LN_EPS) * g + b


def _inproj_kernel(x_ref, w_ref, o_ref, xb_ref, *, bn, q_lo, q_hi, q_scale):
    j = pl.program_id(1)

    @pl.when(j == 0)
    def _():
        xb_ref[...] = x_ref[...].astype(BF16)

    acc = jnp.dot(xb_ref[...], w_ref[...], preferred_element_type=F32)
    col0 = j * bn
    s = jnp.where((col0 >= q_lo) & (col0 < q_hi), q_scale, 1.0).astype(F32)
    o_ref[...] = (acc * s).astype(BF16)


def _inproj(x2, w_main, *, q_lo, q_hi, q_scale, bm=1024, bn=512):
    n, d = x2.shape
    c = w_main.shape[1]
    return pl.pallas_call(
        functools.partial(_inproj_kernel, bn=bn, q_lo=q_lo, q_hi=q_hi, q_scale=q_scale),
        grid=(n // bm, c // bn),
        in_specs=[pl.BlockSpec((bm, d), lambda i, j: (i, 0)),
                  pl.BlockSpec((d, bn), lambda i, j: (0, j))],
        out_specs=pl.BlockSpec((bm, bn), lambda i, j: (i, j)),
        out_shape=jax.ShapeDtypeStruct((n, c), BF16),
        scratch_shapes=[pltpu.VMEM((bm, d), BF16)],
        compiler_params=_cparams(("parallel", "arbitrary")),
        name="inproj",
    )(x2, w_main)


def _forget_kernel(x_ref, wf_ref, b_ref, sel_ref, ka_ref, edge_ref, carry_ref, *, ts, n_heads):
    i = pl.program_id(1)

    @pl.when(i == 0)
    def _():
        carry_ref[...] = jnp.zeros_like(carry_ref)

    z = jnp.dot(x_ref[...].astype(BF16), wf_ref[...], preferred_element_type=F32) + b_ref[...]
    lf = jnp.minimum(z, 0.0) - jnp.log1p(jnp.exp(-jnp.abs(z)))
    lane = lax.broadcasted_iota(I32, lf.shape, 1)
    lf = jnp.where(lane < n_heads, lf, 0.0)

    row = lax.broadcasted_iota(I32, (ts, ts), 0)
    col = lax.broadcasted_iota(I32, (ts, ts), 1)
    tri = jnp.where(col <= row, 1.0, 0.0).astype(BF16)
    hi, mid, lo = _split3(lf)
    cs = (jnp.dot(tri, hi, preferred_element_type=F32)
          + jnp.dot(tri, mid, preferred_element_type=F32)
          + jnp.dot(tri, lo, preferred_element_type=F32)) + carry_ref[...]
    carry_ref[...] = cs[ts - 1:ts, :]
    sub = lax.broadcasted_iota(I32, (8, LANES), 0)
    edge_ref[0] = jnp.where(sub == 0, -cs[0:1, :], -cs[ts - 1:ts, :])

    h2, m2, l2 = _split3(-cs)
    ka = (jnp.dot(h2, sel_ref[0], preferred_element_type=F32)
          + jnp.dot(m2, sel_ref[1], preferred_element_type=F32)
          + jnp.dot(l2, sel_ref[2], preferred_element_type=F32))
    ka_ref[...] = ka.astype(BF16)


def _forget_bias(x2, wf_pad, b_pad, sel, *, batch, seq, n_heads, ts=ATTN_BLOCK):
    n, d = x2.shape
    nt = seq // ts
    return pl.pallas_call(
        functools.partial(_forget_kernel, ts=ts, n_heads=n_heads),
        grid=(batch, nt),
        in_specs=[pl.BlockSpec((ts, d), lambda b, i: (b * nt + i, 0)),
                  pl.BlockSpec((d, LANES), lambda b, i: (0, 0)),
                  pl.BlockSpec((1, LANES), lambda b, i: (0, 0)),
                  pl.BlockSpec((3, LANES, n_heads * LANES), lambda b, i: (0, 0, 0))],
        out_specs=[pl.BlockSpec((ts, n_heads * LANES), lambda b, i: (b * nt + i, 0)),
                   pl.BlockSpec((1, 8, LANES), lambda b, i: (b * nt + i, 0, 0))],
        out_shape=[jax.ShapeDtypeStruct((n, n_heads * LANES), BF16),
                   jax.ShapeDtypeStruct((batch * nt, 8, LANES), F32)],
        scratch_shapes=[pltpu.VMEM((1, LANES), F32)],
        compiler_params=_cparams(("arbitrary", "arbitrary")),
        name="forget_bias",
    )(x2, wf_pad, b_pad, sel)


def _pool_kernel(u_ref, halo_ref, pw_ref, sc_ref, o_ref, buf_ref, *, ts, gdim):
    i = pl.program_id(1)
    h = MAX_WINDOW
    halo = halo_ref[...].astype(F32)
    buf_ref[0:h, :] = jnp.where(i > 0, halo, 0.0)
    buf_ref[h:h + ts, :] = u_ref[...].astype(F32)

    t = i * ts + lax.broadcasted_iota(I32, (ts, 1), 0)
    for g, w in enumerate(POOL_WINDOWS):
        cs = slice(g * gdim, (g + 1) * gdim)
        tok = buf_ref[h:h + ts, cs]
        wsum = tok
        for k in range(1, w):
            wsum = wsum + buf_ref[h - k:h - k + ts, cs]
        cnt = jnp.minimum(t + 1, w).astype(F32)
        p = (wsum / cnt - tok).astype(BF16)
        y = jnp.dot(p, pw_ref[g], preferred_element_type=F32) * sc_ref[:, cs]
        o_ref[:, cs] = y.astype(BF16)


def _pool(z, pool_w_b, pool_scale2, *, batch, seq, width, ts=512):
    n = z.shape[0]
    groups = len(POOL_WINDOWS)
    gdim = width // groups
    nt = seq // ts
    hb = ts // MAX_WINDOW
    return pl.pallas_call(
        functools.partial(_pool_kernel, ts=ts, gdim=gdim),
        grid=(batch, nt),
        in_specs=[pl.BlockSpec((ts, width), lambda b, i: (b * nt + i, 0)),
                  pl.BlockSpec((MAX_WINDOW, width),
                               lambda b, i: (jnp.maximum((b * nt + i) * hb - 1, 0), 0)),
                  pl.BlockSpec((groups, gdim, gdim), lambda b, i: (0, 0, 0)),
                  pl.BlockSpec((1, width), lambda b, i: (0, 0))],
        out_specs=pl.BlockSpec((ts, width), lambda b, i: (b * nt + i, 0)),
        out_shape=jax.ShapeDtypeStruct((n, width), BF16),
        scratch_shapes=[pltpu.VMEM((ts + MAX_WINDOW, width), F32)],
        compiler_params=_cparams(("parallel", "parallel")),
        name="pool_mixer",
    )(z, z, pool_w_b, pool_scale2)


def _attn_kernel(bfirst_ref, blast_ref, q_ref, k_ref, ka_ref, v_ref, o_ref,
                 qa_ref, m_ref, l_ref, acc_ref, kmax_ref, *, blk, n_kblk):
    qi = pl.program_id(2)
    base = (pl.program_id(0) * pl.num_programs(1) + pl.program_id(1)) * n_kblk

    @pl.when(qi == 0)
    def _():
        def knorm(c, run):
            kk = k_ref[pl.ds(pl.multiple_of(c * blk, blk), blk), :].astype(F32)
            return jnp.maximum(run, jnp.sum(kk * kk, axis=1, keepdims=True))

        run = lax.fori_loop(0, n_kblk, knorm, jnp.zeros((blk, 1), F32))
        kmax_ref[0] = jnp.max(jnp.sqrt(run))

    qq = q_ref[...].astype(F32)
    qmax = jnp.max(jnp.sqrt(jnp.sum(qq * qq, axis=1, keepdims=True)))
    thr = bfirst_ref[base + Q_HALVES * qi] - SKIP_MARGIN - 2.0 * qmax * kmax_ref[0]

    def count_skippable(j, a):
        return a + jnp.where(blast_ref[base + j] < thr, 1, 0).astype(I32)

    n_skip = lax.fori_loop(0, Q_HALVES * qi, count_skippable, jnp.int32(0))

    lane = lax.broadcasted_iota(I32, (blk, HEAD_DIM), 1)
    ones3 = jnp.where(lane < 3, 1.0, 0.0).astype(BF16)
    for h in range(Q_HALVES):
        qa_ref[h] = jnp.concatenate([q_ref[h * blk:(h + 1) * blk, :], ones3], axis=1)
    m_ref[...] = jnp.full_like(m_ref, -jnp.inf)
    l_ref[...] = jnp.zeros_like(l_ref)
    acc_ref[...] = jnp.zeros_like(acc_ref)
    reps = blk // LANES

    def step(h, ki, masked):
        ks = pl.multiple_of(ki * blk, blk)
        k_aug = jnp.concatenate([k_ref[pl.ds(ks, blk), :], ka_ref[pl.ds(ks, blk), :]], axis=1)
        s = lax.dot_general(qa_ref[h], k_aug, (((1,), (1,)), ((), ())),
                            preferred_element_type=F32)
        if masked:
            r = lax.broadcasted_iota(I32, (blk, blk), 0)
            c = lax.broadcasted_iota(I32, (blk, blk), 1)
            s = jnp.where(c <= r, s, -jnp.inf)
        m_prev = m_ref[h]
        m_next = jnp.maximum(m_prev, jnp.max(s, axis=1, keepdims=True))
        alpha = jnp.exp(m_prev - m_next)
        p = jnp.exp(s - jnp.concatenate([m_next] * reps, axis=1))
        l_ref[h] = alpha * l_ref[h] + jnp.sum(p, axis=1, keepdims=True)
        acc_ref[h] = alpha * acc_ref[h] + jnp.dot(
            p.astype(BF16), v_ref[pl.ds(ks, blk), :], preferred_element_type=F32)
        m_ref[h] = m_next

    def body(kq, carry):
        for d in range(Q_HALVES):
            for h in range(Q_HALVES):
                step(h, Q_HALVES * kq + d, False)
        return carry

    lax.fori_loop(n_skip // Q_HALVES, qi, body, 0)
    for h in range(Q_HALVES):
        for d in range(h):
            step(h, Q_HALVES * qi + d, False)
        step(h, Q_HALVES * qi + h, True)
        o_ref[h * blk:(h + 1) * blk, :] = (acc_ref[h] / l_ref[h]).astype(BF16)


def _attention(bfirst, blast, z, ka, *, batch, seq, n_heads, q_col, k_col, v_col, blk=ATTN_BLOCK):
    n = z.shape[0]
    qb = Q_HALVES * blk
    nq = seq // qb
    grid_spec = pltpu.PrefetchScalarGridSpec(
        num_scalar_prefetch=2,
        grid=(batch, n_heads, nq),
        in_specs=[pl.BlockSpec((qb, HEAD_DIM), lambda b, h, i, bf, bl: (b * nq + i, q_col + h)),
                  pl.BlockSpec((seq, HEAD_DIM), lambda b, h, i, bf, bl: (b, k_col + h)),
                  pl.BlockSpec((seq, HEAD_DIM), lambda b, h, i, bf, bl: (b, h)),
                  pl.BlockSpec((seq, HEAD_DIM), lambda b, h, i, bf, bl: (b, v_col + h))],
        out_specs=pl.BlockSpec((qb, HEAD_DIM), lambda b, h, i, bf, bl: (b * nq + i, h)),
        scratch_shapes=[pltpu.VMEM((Q_HALVES, blk, 2 * HEAD_DIM), BF16),
                        pltpu.VMEM((Q_HALVES, blk, LANES), F32),
                        pltpu.VMEM((Q_HALVES, blk, LANES), F32),
                        pltpu.VMEM((Q_HALVES, blk, HEAD_DIM), F32),
                        pltpu.SMEM((1,), F32)],
    )
    return pl.pallas_call(
        functools.partial(_attn_kernel, blk=blk, n_kblk=seq // blk),
        grid_spec=grid_spec,
        out_shape=jax.ShapeDtypeStruct((n, n_heads * HEAD_DIM), BF16),
        compiler_params=_cparams(("arbitrary", "arbitrary", "arbitrary")),
        name="fox_attention",
    )(bfirst, blast, z, z, ka, z)


def _merge_kernel(x_ref, p_ref, a_ref, wg0_ref, wg1_ref, wbp_ref, wba_ref, o_ref, xb_ref):
    j = pl.program_id(1)

    @pl.when(j == 0)
    def _():
        xb_ref[...] = x_ref[...].astype(BF16)

    xb = xb_ref[...]
    g0 = jax.nn.sigmoid(jnp.dot(xb, wg0_ref[...], preferred_element_type=F32))
    g1 = jax.nn.sigmoid(jnp.dot(xb, wg1_ref[...], preferred_element_type=F32))
    yp = jnp.dot(p_ref[...], wbp_ref[...], preferred_element_type=F32)
    ya = jnp.dot(a_ref[...], wba_ref[...], preferred_element_type=F32)
    o_ref[...] = (g0 * yp + g1 * ya).astype(BF16)


def _merge(x2, pool_o, attn_o, wg, wbp, wba, *, bm=1024, bn=512):
    n, d = x2.shape
    wp = pool_o.shape[1]
    wa = attn_o.shape[1]
    nj = d // bn
    return pl.pallas_call(
        _merge_kernel,
        grid=(n // bm, nj),
        in_specs=[pl.BlockSpec((bm, d), lambda i, j: (i, 0)),
                  pl.BlockSpec((bm, wp), lambda i, j: (i, 0)),
                  pl.BlockSpec((bm, wa), lambda i, j: (i, 0)),
                  pl.BlockSpec((d, bn), lambda i, j: (0, j)),
                  pl.BlockSpec((d, bn), lambda i, j: (0, nj + j)),
                  pl.BlockSpec((wp, bn), lambda i, j: (0, j)),
                  pl.BlockSpec((wa, bn), lambda i, j: (0, j))],
        out_specs=pl.BlockSpec((bm, bn), lambda i, j: (i, j)),
        out_shape=jax.ShapeDtypeStruct((n, d), BF16),
        scratch_shapes=[pltpu.VMEM((bm, d), BF16)],
        compiler_params=_cparams(("parallel", "arbitrary")),
        name="branch_merge",
    )(x2, pool_o, attn_o, wg, wg, wbp, wba)


def _out_ln_router_kernel(x_ref, m_ref, wo_ref, g_ref, b_ref, wrh_ref, wrl_ref,
                          x1_ref, x1b_ref, lg_ref):
    mix = jnp.dot(m_ref[...], wo_ref[...], preferred_element_type=F32)
    x1 = _layer_norm(ALPHA * x_ref[...] + mix, g_ref[...], b_ref[...])
    x1_ref[...] = x1
    hi = x1.astype(BF16)
    x1b_ref[...] = hi
    lo = (x1 - hi.astype(F32)).astype(BF16)
    lg_ref[...] = (jnp.dot(hi, wrh_ref[...], preferred_element_type=F32)
                   + jnp.dot(lo, wrh_ref[...], preferred_element_type=F32)
                   + jnp.dot(hi, wrl_ref[...], preferred_element_type=F32))


def _out_ln_router(x2, merged, wo, g1, b1, wr_hi, wr_lo, *, bm=512):
    n, d = x2.shape
    row = lambda i: (i, 0)
    fix = lambda i: (0, 0)
    return pl.pallas_call(
        _out_ln_router_kernel,
        grid=(n // bm,),
        in_specs=[pl.BlockSpec((bm, d), row), pl.BlockSpec((bm, d), row),
                  pl.BlockSpec((d, d), fix), pl.BlockSpec((1, d), fix), pl.BlockSpec((1, d), fix),
                  pl.BlockSpec((d, LANES), fix), pl.BlockSpec((d, LANES), fix)],
        out_specs=[pl.BlockSpec((bm, d), row), pl.BlockSpec((bm, d), row),
                   pl.BlockSpec((bm, LANES), row)],
        out_shape=[jax.ShapeDtypeStruct((n, d), F32), jax.ShapeDtypeStruct((n, d), BF16),
                   jax.ShapeDtypeStruct((n, LANES), F32)],
        compiler_params=_cparams(("parallel",)),
        name="outproj_ln1_router",
    )(x2, merged, wo, g1, b1, wr_hi, wr_lo)


def _route_kernel(lg_ref, bias_ref, idx_ref, gate_ref, rank_ref, cnt_ref, carry_ref,
                  *, bm, n_experts):
    i = pl.program_id(0)

    @pl.when(i == 0)
    def _():
        carry_ref[...] = jnp.zeros_like(carry_ref)

    lane = lax.broadcasted_iota(I32, (bm, LANES), 1)
    lane_f = lane.astype(F32)
    scores = jax.nn.sigmoid(lg_ref[...])
    sel = jnp.where(lane < n_experts, scores + bias_ref[...], -jnp.inf)

    hits, gates = [], []
    gsum = jnp.zeros((bm, 1), F32)
    member = jnp.zeros((bm, LANES), F32)
    idx_out = jnp.zeros((bm, LANES), F32)
    for k in range(TOP_K):
        m = jnp.max(sel, axis=1, keepdims=True)
        ik = jnp.min(jnp.where(sel == m, lane_f, float(LANES)), axis=1, keepdims=True)
        hit = lane_f == ik
        gk = jnp.sum(jnp.where(hit, scores, 0.0), axis=1, keepdims=True)
        sel = jnp.where(hit, -jnp.inf, sel)
        member = jnp.where(hit, 1.0, member)
        idx_out = jnp.where(lane == k, ik, idx_out)
        gsum = gsum + gk
        hits.append(hit)
        gates.append(gk)

    row = lax.broadcasted_iota(I32, (bm, bm), 0)
    col = lax.broadcasted_iota(I32, (bm, bm), 1)
    before = jnp.where(col < row, 1.0, 0.0).astype(BF16)
    prefix = jnp.dot(before, member.astype(BF16), preferred_element_type=F32) + carry_ref[...]

    gate_out = jnp.zeros((bm, LANES), F32)
    rank_out = jnp.zeros((bm, LANES), F32)
    for k in range(TOP_K):
        rk = jnp.sum(jnp.where(hits[k], prefix, 0.0), axis=1, keepdims=True)
        rank_out = jnp.where(lane == k, rk, rank_out)
        gate_out = jnp.where(lane == k, gates[k] / gsum * ROUTE_SCALE, gate_out)

    total = carry_ref[...] + jnp.sum(member, axis=0, keepdims=True)
    carry_ref[...] = total
    idx_ref[...] = idx_out.astype(I32)
    gate_ref[...] = gate_out
    rank_ref[...] = rank_out.astype(I32)
    cnt_ref[...] = jnp.broadcast_to(total, cnt_ref.shape).astype(I32)


def _route(logits, bias_pad, *, n_experts, bm=TOK_TILE):
    n = logits.shape[0]
    row = lambda i: (i, 0)
    fix = lambda i: (0, 0)
    return pl.pallas_call(
        functools.partial(_route_kernel, bm=bm, n_experts=n_experts),
        grid=(n // bm,),
        in_specs=[pl.BlockSpec((bm, LANES), row), pl.BlockSpec((1, LANES), fix)],
        out_specs=[pl.BlockSpec((bm, LANES), row), pl.BlockSpec((bm, LANES), row),
                   pl.BlockSpec((bm, LANES), row), pl.BlockSpec((8, LANES), fix)],
        out_shape=[jax.ShapeDtypeStruct((n, LANES), I32), jax.ShapeDtypeStruct((n, LANES), F32),
                   jax.ShapeDtypeStruct((n, LANES), I32), jax.ShapeDtypeStruct((8, LANES), I32)],
        scratch_shapes=[pltpu.VMEM((1, LANES), F32)],
        compiler_params=_cparams(("arbitrary",)),
        name="route_topk",
    )(logits, bias_pad)


def _dispatch_kernel(zs_ref, dest_ref, x_ref, xs_hbm, zero_ref, sem, zsem, *, bt, n_experts):
    i = pl.program_id(0)

    def zero_copy(e):
        zs = pl.multiple_of(zs_ref[e], SUB_ROWS)
        return pltpu.make_async_copy(zero_ref, xs_hbm.at[pl.ds(zs, SUB_ROWS), :], zsem)

    @pl.when(i == 0)
    def _():
        zero_ref[...] = jnp.zeros_like(zero_ref)

        def zstart(e, c):
            zero_copy(e).start()
            return c

        def zwait(e, c):
            zero_copy(e).wait()
            return c

        lax.fori_loop(0, n_experts, zstart, 0)
        lax.fori_loop(0, n_experts, zwait, 0)

    def row_copy(t, k):
        dst = dest_ref[0, 0, t * TOP_K + k]
        return pltpu.make_async_copy(x_ref.at[pl.ds(t, 1), :], xs_hbm.at[pl.ds(dst, 1), :], sem)

    def start(t, c):
        for k in range(TOP_K):
            row_copy(t, k).start()
        return c

    def wait(t, c):
        for k in range(TOP_K):
            row_copy(t, k).wait()
        return c

    lax.fori_loop(0, bt, start, 0)
    lax.fori_loop(0, bt, wait, 0)


def _dispatch(zstart, dest3, x1, *, rows_total, n_experts, bt=TOK_TILE):
    n, d = x1.shape
    grid_spec = pltpu.PrefetchScalarGridSpec(
        num_scalar_prefetch=1,
        grid=(n // bt,),
        in_specs=[pl.BlockSpec((1, 1, bt * TOP_K), lambda i, zs: (i, 0, 0),
                               memory_space=pltpu.SMEM),
                  pl.BlockSpec((bt, d), lambda i, zs: (i, 0))],
        out_specs=pl.BlockSpec(memory_space=pl.ANY),
        scratch_shapes=[pltpu.VMEM((SUB_ROWS, d), F32),
                        pltpu.SemaphoreType.DMA(()),
                        pltpu.SemaphoreType.DMA(())],
    )
    return pl.pallas_call(
        functools.partial(_dispatch_kernel, bt=bt, n_experts=n_experts),
        grid_spec=grid_spec,
        out_shape=jax.ShapeDtypeStruct((rows_total, d), F32),
        compiler_params=_cparams(("arbitrary",)),
        name="dispatch_rows",
    )(zstart, dest3, x1)


def _experts_kernel(ce_ref, cv_ref, tot_ref, x_ref, w1_ref, w3_ref, w2_ref, o_ref,
                    w13_ref, acc_ref, h_ref, w2b_ref, *, n_up, fdim):
    c = pl.program_id(0)
    j = pl.program_id(1)
    nv = cv_ref[c]
    del ce_ref, tot_ref
    all_rows = slice(0, ROW_CHUNK)
    sub_blocks = [slice(sb * SUB_ROWS, (sb + 1) * SUB_ROWS) for sb in range(ROW_CHUNK // SUB_ROWS)]

    def up_rows(rows, first, last):
        part = jnp.dot(x_ref[rows, :].astype(BF16), w13_ref[...], preferred_element_type=F32)
        if not first:
            part = part + acc_ref[rows, :]
        if last:
            h1 = part[:, :fdim]
            h_ref[rows, :] = (h1 * jax.nn.sigmoid(h1) * part[:, fdim:]).astype(BF16)
        else:
            acc_ref[rows, :] = part

    def cast_up_weights():
        w13_ref[:, :fdim] = w1_ref[...].astype(BF16)
        w13_ref[:, fdim:] = w3_ref[...].astype(BF16)

    def up(first, last):
        @pl.when(nv == ROW_CHUNK)
        def _():
            cast_up_weights()
            up_rows(all_rows, first, last)

        @pl.when(nv < ROW_CHUNK)
        def _():
            cast_up_weights()
            for rows in sub_blocks:
                @pl.when(rows.start < nv)
                def _():
                    up_rows(rows, first, last)

    @pl.when(j == 0)
    def _():
        up(True, n_up == 1)

    if n_up > 2:
        @pl.when((j > 0) & (j < n_up - 1))
        def _():
            up(False, False)

    if n_up > 1:
        @pl.when(j == n_up - 1)
        def _():
            up(False, True)

    @pl.when(j >= n_up)
    def _():
        @pl.when(nv == ROW_CHUNK)
        def _():
            w2b_ref[...] = w2_ref[...].astype(BF16)
            o_ref[...] = jnp.dot(h_ref[...], w2b_ref[...], preferred_element_type=F32)

        @pl.when(nv < ROW_CHUNK)
        def _():
            w2b_ref[...] = w2_ref[...].astype(BF16)
            for rows in sub_blocks:
                @pl.when(rows.start < nv)
                def _():
                    o_ref[rows, :] = jnp.dot(h_ref[rows, :], w2b_ref[...],
                                             preferred_element_type=F32)

                @pl.when(rows.start >= nv)
                def _():
                    o_ref[rows, :] = jnp.zeros((SUB_ROWS, o_ref.shape[1]), F32)


def _experts(chunk_expert, chunk_valid, total, xs, w1, w3, w2, *, n_chunks):
    d = xs.shape[1]
    fdim = w1.shape[2]
    n_up = d // UP_K_TILE
    n_down = d // DOWN_N_TILE

    def x_map(c, j, ce, cv, tot):
        return (jnp.minimum(c, tot[0] - 1), jnp.minimum(j, n_up - 1))

    def w13_map(c, j, ce, cv, tot):
        return (ce[c], jnp.minimum(j, n_up - 1), 0)

    def w2_map(c, j, ce, cv, tot):
        return (ce[c], 0, jnp.maximum(j - n_up, 0))

    def o_map(c, j, ce, cv, tot):
        return (c, jnp.maximum(j - n_up, 0))

    grid_spec = pltpu.PrefetchScalarGridSpec(
        num_scalar_prefetch=3,
        grid=(n_chunks, n_up + n_down),
        in_specs=[pl.BlockSpec((ROW_CHUNK, UP_K_TILE), x_map),
                  pl.BlockSpec((None, UP_K_TILE, fdim), w13_map),
                  pl.BlockSpec((None, UP_K_TILE, fdim), w13_map),
                  pl.BlockSpec((None, fdim, DOWN_N_TILE), w2_map)],
        out_specs=pl.BlockSpec((ROW_CHUNK, DOWN_N_TILE), o_map),
        scratch_shapes=[pltpu.VMEM((UP_K_TILE, 2 * fdim), BF16),
                        pltpu.VMEM((ROW_CHUNK, 2 * fdim), F32),
                        pltpu.VMEM((ROW_CHUNK, fdim), BF16),
                        pltpu.VMEM((fdim, DOWN_N_TILE), BF16)],
    )
    return pl.pallas_call(
        functools.partial(_experts_kernel, n_up=n_up, fdim=fdim),
        grid_spec=grid_spec,
        out_shape=jax.ShapeDtypeStruct((n_chunks * ROW_CHUNK, d), F32),
        compiler_params=_cparams(("arbitrary", "arbitrary")),
        name="routed_experts",
    )(chunk_expert, chunk_valid, total, xs, w1, w3, w2)


def _shared_kernel(x_ref, w13_ref, w2_ref, o_ref, *, tf):
    f = pl.program_id(1)
    r = jnp.dot(x_ref[...], w13_ref[...], preferred_element_type=F32)
    h1 = r[:, :tf]
    h = (h1 * jax.nn.sigmoid(h1) * r[:, tf:]).astype(BF16)
    y = jnp.dot(h, w2_ref[...], preferred_element_type=F32)

    @pl.when(f == 0)
    def _():
        o_ref[...] = y

    @pl.when(f > 0)
    def _():
        o_ref[...] += y


def _shared_expert(x1b, w13s, w2s, *, bm=1024, tf=SHARED_F_TILE):
    n, d = x1b.shape
    nf = w2s.shape[0] // tf
    return pl.pallas_call(
        functools.partial(_shared_kernel, tf=tf),
        grid=(n // bm, nf),
        in_specs=[pl.BlockSpec((bm, d), lambda i, f: (i, 0)),
                  pl.BlockSpec((d, 2 * tf), lambda i, f: (0, f)),
                  pl.BlockSpec((tf, d), lambda i, f: (f, 0))],
        out_specs=pl.BlockSpec((bm, d), lambda i, f: (i, 0)),
        out_shape=jax.ShapeDtypeStruct((n, d), F32),
        compiler_params=_cparams(("parallel", "arbitrary")),
        name="shared_expert",
    )(x1b, w13s, w2s)


def _combine_kernel(dcur_ref, dnext_ref, x1_ref, ysh_ref, gate_ref, g_ref, b_ref, ys_hbm, o_ref,
                    buf_ref, sem, *, bt):
    i = pl.program_id(0)
    slot = lax.rem(i, 2)

    def row_copy(dref, s, t, k):
        src = dref[0, 0, t * TOP_K + k]
        return pltpu.make_async_copy(ys_hbm.at[pl.ds(src, 1), :],
                                     buf_ref.at[s, k, pl.ds(t, 1), :], sem.at[s])

    def issue(dref, s):
        def start(t, c):
            for k in range(TOP_K):
                row_copy(dref, s, t, k).start()
            return c

        lax.fori_loop(0, bt, start, 0)

    @pl.when(i == 0)
    def _():
        issue(dcur_ref, slot)

    @pl.when(i + 1 < pl.num_programs(0))
    def _():
        issue(dnext_ref, 1 - slot)

    acc = ALPHA * x1_ref[...] + ysh_ref[...]

    def wait(t, c):
        for k in range(TOP_K):
            row_copy(dcur_ref, slot, t, k).wait()
        return c

    lax.fori_loop(0, bt, wait, 0)
    gate = gate_ref[...]
    for k in range(TOP_K):
        acc = acc + gate[:, k:k + 1] * buf_ref[slot, k]
    o_ref[...] = _layer_norm(acc, g_ref[...], b_ref[...])


def _combine(dest3, x1, ysh, gate, g2, b2, ys, *, bt=COMBINE_TILE):
    n, d = x1.shape
    nt = n // bt
    row = lambda i: (i, 0)
    fix = lambda i: (0, 0)
    return pl.pallas_call(
        functools.partial(_combine_kernel, bt=bt),
        grid=(nt,),
        in_specs=[pl.BlockSpec((1, 1, bt * TOP_K), lambda i: (i, 0, 0), memory_space=pltpu.SMEM),
                  pl.BlockSpec((1, 1, bt * TOP_K), lambda i: (jnp.minimum(i + 1, nt - 1), 0, 0),
                               memory_space=pltpu.SMEM),
                  pl.BlockSpec((bt, d), row), pl.BlockSpec((bt, d), row),
                  pl.BlockSpec((bt, LANES), row),
                  pl.BlockSpec((1, d), fix), pl.BlockSpec((1, d), fix),
                  pl.BlockSpec(memory_space=pl.ANY)],
        out_specs=pl.BlockSpec((bt, d), row),
        out_shape=jax.ShapeDtypeStruct((n, d), F32),
        scratch_shapes=[pltpu.VMEM((2, TOP_K, bt, d), F32), pltpu.SemaphoreType.DMA((2,))],
        compiler_params=_cparams(("arbitrary",)),
        name="combine_ln2",
    )(dest3, dest3, x1, ysh, gate, g2, b2, ys)


def _layer(x, w_in, b_forget, pool_w, pool_scale, w_branch_pool, w_branch_attn, w_out,
           ln1_g, ln1_b, w_router, router_bias, w1, w3, w2, w_shared1, w_shared3, w_shared2,
           ln2_g, ln2_b):
    batch, seq, d = x.shape
    n = batch * seq
    pool_width = w_branch_pool.shape[0]
    attn_width = w_branch_attn.shape[0]
    n_heads = attn_width // HEAD_DIM
    n_experts = w1.shape[0]
    off_q = pool_width
    off_k = off_q + attn_width
    off_v = off_k + attn_width
    off_f = off_v + attn_width
    off_gate = off_f + n_heads

    x2 = x.reshape(n, d)

    w_main = w_in[:, :off_f].astype(BF16)
    wf_pad = jnp.pad(w_in[:, off_f:off_gate], ((0, 0), (0, LANES - n_heads))).astype(BF16)
    bf_pad = jnp.pad(b_forget.astype(F32), (0, LANES - n_heads)).reshape(1, LANES)
    w_gate = w_in[:, off_gate:].astype(BF16)
    piece = jnp.arange(3)[:, None, None]
    src = jnp.arange(LANES)[None, :, None]
    dst = jnp.arange(n_heads * LANES)[None, None, :]
    sel = ((src < n_heads) & (dst == src * LANES + piece)).astype(BF16)
    wr = jnp.pad(w_router.astype(F32), ((0, 0), (0, LANES - n_experts)))
    wr_hi = wr.astype(BF16)
    wr_lo = (wr - wr_hi.astype(F32)).astype(BF16)
    rb_pad = jnp.pad(router_bias.astype(F32), (0, LANES - n_experts)).reshape(1, LANES)
    tf = SHARED_F_TILE
    fs = w_shared1.shape[1]
    w13s = jnp.concatenate([w_shared1.reshape(d, fs // tf, tf), w_shared3.reshape(d, fs // tf, tf)],
                           axis=2).reshape(d, 2 * fs).astype(BF16)

    z = _inproj(x2, w_main, q_lo=off_q, q_hi=off_k, q_scale=HEAD_DIM ** -0.5)
    ka, edge = _forget_bias(x2, wf_pad, bf_pad, sel, batch=batch, seq=seq, n_heads=n_heads)
    n_kblk = seq // ATTN_BLOCK
    edge = edge.reshape(batch, n_kblk, 8, LANES)[:, :, :2, :n_heads]
    bfirst = edge[:, :, 0, :].transpose(0, 2, 1).reshape(-1)
    blast = edge[:, :, 1, :].transpose(0, 2, 1).reshape(-1)
    pool_o = _pool(z, pool_w.astype(BF16), pool_scale.astype(F32).reshape(1, pool_width),
                   batch=batch, seq=seq, width=pool_width)
    attn_o = _attention(bfirst, blast, z, ka, batch=batch, seq=seq, n_heads=n_heads,
                        q_col=off_q // HEAD_DIM, k_col=off_k // HEAD_DIM, v_col=off_v // HEAD_DIM)
    merged = _merge(x2, pool_o, attn_o, w_gate, w_branch_pool.astype(BF16),
                    w_branch_attn.astype(BF16))
    x1, x1b, logits = _out_ln_router(x2, merged, w_out.astype(BF16),
                                     ln1_g.astype(F32).reshape(1, d), ln1_b.astype(F32).reshape(1, d),
                                     wr_hi, wr_lo)

    idx, gate, rank, cnt = _route(logits, rb_pad, n_experts=n_experts)
    counts = cnt[0, :n_experts]
    n_assign = n * TOP_K
    n_chunks = -(-n_assign // ROW_CHUNK) + n_experts
    chunks_per = (counts + ROW_CHUNK - 1) // ROW_CHUNK
    chunk_end = jnp.cumsum(chunks_per)
    chunk_begin = chunk_end - chunks_per
    total = chunk_end[-1]
    group_start = chunk_begin * ROW_CHUNK
    cids = jnp.arange(n_chunks, dtype=I32)
    c_eff = jnp.minimum(cids, total - 1)
    chunk_expert = jnp.minimum(jnp.sum((chunk_end[None, :] <= c_eff[:, None]).astype(I32), axis=1),
                               n_experts - 1).astype(I32)
    chunk_valid = jnp.where(cids < total,
                            jnp.minimum(counts[chunk_expert] - (cids - chunk_begin[chunk_expert]) * ROW_CHUNK,
                                        ROW_CHUNK), 0).astype(I32)
    idx_k = idx[:, :TOP_K]
    dest = group_start[idx_k].astype(I32) + rank[:, :TOP_K]
    dest3 = dest.reshape(n // TOK_TILE, 1, TOK_TILE * TOP_K)
    zstart = ((group_start + counts) // SUB_ROWS * SUB_ROWS).astype(I32)
    rows_total = (n_chunks + 1) * ROW_CHUNK

    xs = _dispatch(zstart, dest3, x1, rows_total=rows_total, n_experts=n_experts)
    ys = _experts(chunk_expert, chunk_valid, total.reshape(1).astype(I32), xs, w1, w3, w2,
                  n_chunks=n_chunks)
    ysh = _shared_expert(x1b, w13s, w_shared2.astype(BF16))
    dest_c = dest.reshape(n // COMBINE_TILE, 1, COMBINE_TILE * TOP_K)
    out = _combine(dest_c, x1, ysh, gate, ln2_g.astype(F32).reshape(1, d),
                   ln2_b.astype(F32).reshape(1, d), ys)
    return out.reshape(batch, seq, d)


def kernel(x, w_in, b_forget, pool_w, pool_scale, w_branch_pool, w_branch_attn, w_out, ln1_g, ln1_b,
           w_router, router_bias, w1, w3, w2, w_shared1, w_shared3, w_shared2, ln2_g, ln2_b):
    for l in range(w_in.shape[0]):
        x = _layer(x, w_in[l], b_forget[l], pool_w[l], pool_scale[l], w_branch_pool[l],
                   w_branch_attn[l], w_out[l], ln1_g[l], ln1_b[l], w_router[l], router_bias[l],
                   w1[l], w3[l], w2[l], w_shared1[l], w_shared3[l], w_shared2[l], ln2_g[l], ln2_b[l])
    return x
```

```python
import functools

import jax
import jax.numpy as jnp
from jax import lax
from jax.experimental import pallas as pl
from jax.experimental.pallas import tpu as pltpu

F32 = jnp.float32
BF16 = jnp.bfloat16
I32 = jnp.int32

POOL_WINDOWS = (2, 4, 8, 16)
HEAD_DIM = 128
TOP_K = 6
ROUTE_SCALE = 1.0
DEPTH = 1
ALPHA = (2 * DEPTH) ** 0.25
LN_EPS = 1e-5

LANES = 128
MAX_WINDOW = max(POOL_WINDOWS)
ATTN_BLOCK = 512
Q_HALVES = 2
SKIP_MARGIN = 110.0
ROW_CHUNK = 1024
SUB_ROWS = 256
PACK_TILE = 512
UP_K_TILE = PACK_TILE
DOWN_N_TILE = PACK_TILE
SHARED_F_TILE = 256
TOK_TILE = 256
VMEM_LIMIT = 56 * 1024 * 1024


def _cparams(sem, vmem=VMEM_LIMIT):
    return pltpu.CompilerParams(dimension_semantics=sem, vmem_limit_bytes=vmem)


def _split3(v):
    hi = v.astype(BF16)
    r1 = v - hi.astype(F32)
    mid = r1.astype(BF16)
    lo = (r1 - mid.astype(F32)).astype(BF16)
    return hi, mid, lo


def _pack_bf16_pairs(v):
    c = v.shape[1] // 2
    lo = lax.bitcast_convert_type(v[:, :c].astype(BF16).astype(F32), I32)
    hi = lax.bitcast_convert_type(v[:, c:].astype(BF16).astype(F32), I32)
    return ((lo >> 16) & 0xFFFF) | (hi & -65536)


def _unpack_bf16_pairs(w):
    lo = lax.bitcast_convert_type(w << 16, F32)
    hi = lax.bitcast_convert_type(w & -65536, F32)
    return jnp.concatenate([lo, hi], axis=1)


def _layer_norm(v, g, b):
    mu = jnp.mean(v, axis=-1, keepdims=True)
    d = v - mu
    var = jnp.mean(d * d, axis=-1, keepdims=True)
    return d * lax.rsqrt(var + LN_EPS) * g + b


def _inproj_kernel(x_ref, w_ref, o_ref, xb_ref, *, bn, q_lo, q_hi, q_scale):
    j = pl.program_id(1)

    @pl.when(j == 0)
    def _():
        xb_ref[...] = x_ref[...].astype(BF16)

    acc = jnp.dot(xb_ref[...], w_ref[...], preferred_element_type=F32)
    col0 = j * bn
    s = jnp.where((col0 >= q_lo) & (col0 < q_hi), q_scale, 1.0).astype(F32)
    o_ref[...] = (acc * s).astype(BF16)


def _inproj(x2, w_main, *, q_lo, q_hi, q_scale, bm=1024, bn=512):
    n, d = x2.shape
    c = w_main.shape[1]
    return pl.pallas_call(
        functools.partial(_inproj_kernel, bn=bn, q_lo=q_lo, q_hi=q_hi, q_scale=q_scale),
        grid=(n // bm, c // bn),
        in_specs=[pl.BlockSpec((bm, d), lambda i, j: (i, 0)),
                  pl.BlockSpec((d, bn), lambda i, j: (0, j))],
        out_specs=pl.BlockSpec((bm, bn), lambda i, j: (i, j)),
        out_shape=jax.ShapeDtypeStruct((n, c), BF16),
        scratch_shapes=[pltpu.VMEM((bm, d), BF16)],
        compiler_params=_cparams(("parallel", "arbitrary")),
        name="inproj",
    )(x2, w_main)


def _forget_kernel(x_ref, wf_ref, b_ref, sel_ref, ka_ref, edge_ref, carry_ref, *, ts, n_heads):
    i = pl.program_id(1)

    @pl.when(i == 0)
    def _():
        carry_ref[...] = jnp.zeros_like(carry_ref)

    z = jnp.dot(x_ref[...].astype(BF16), wf_ref[...], preferred_element_type=F32) + b_ref[...]
    lf = jnp.minimum(z, 0.0) - jnp.log1p(jnp.exp(-jnp.abs(z)))
    lane = lax.broadcasted_iota(I32, lf.shape, 1)
    lf = jnp.where(lane < n_heads, lf, 0.0)

    row = lax.broadcasted_iota(I32, (ts, ts), 0)
    col = lax.broadcasted_iota(I32, (ts, ts), 1)
    tri = jnp.where(col <= row, 1.0, 0.0).astype(BF16)
    hi, mid, lo = _split3(lf)
    cs = (jnp.dot(tri, hi, preferred_element_type=F32)
          + jnp.dot(tri, mid, preferred_element_type=F32)
          + jnp.dot(tri, lo, preferred_element_type=F32)) + carry_ref[...]
    carry_ref[...] = cs[ts - 1:ts, :]
    sub = lax.broadcasted_iota(I32, (8, LANES), 0)
    edge_ref[0] = jnp.where(sub == 0, -cs[0:1, :], -cs[ts - 1:ts, :])

    h2, m2, l2 = _split3(-cs)
    ka = (jnp.dot(h2, sel_ref[0], preferred_element_type=F32)
          + jnp.dot(m2, sel_ref[1], preferred_element_type=F32)
          + jnp.dot(l2, sel_ref[2], preferred_element_type=F32))
    ka_ref[...] = ka.astype(BF16)


def _forget_bias(x2, wf_pad, b_pad, sel, *, batch, seq, n_heads, ts=ATTN_BLOCK):
    n, d = x2.shape
    nt = seq // ts
    return pl.pallas_call(
        functools.partial(_forget_kernel, ts=ts, n_heads=n_heads),
        grid=(batch, nt),
        in_specs=[pl.BlockSpec((ts, d), lambda b, i: (b * nt + i, 0)),
                  pl.BlockSpec((d, LANES), lambda b, i: (0, 0)),
                  pl.BlockSpec((1, LANES), lambda b, i: (0, 0)),
                  pl.BlockSpec((3, LANES, n_heads * LANES), lambda b, i: (0, 0, 0))],
        out_specs=[pl.BlockSpec((ts, n_heads * LANES), lambda b, i: (b * nt + i, 0)),
                   pl.BlockSpec((1, 8, LANES), lambda b, i: (b * nt + i, 0, 0))],
        out_shape=[jax.ShapeDtypeStruct((n, n_heads * LANES), BF16),
                   jax.ShapeDtypeStruct((batch * nt, 8, LANES), F32)],
        scratch_shapes=[pltpu.VMEM((1, LANES), F32)],
        compiler_params=_cparams(("arbitrary", "arbitrary")),
        name="forget_bias",
    )(x2, wf_pad, b_pad, sel)


def _pool_kernel(u_ref, halo_ref, pw_ref, sc_ref, o_ref, buf_ref, *, ts, gdim):
    i = pl.program_id(1)
    h = MAX_WINDOW
    halo = halo_ref[...].astype(F32)
    buf_ref[0:h, :] = jnp.where(i > 0, halo, 0.0)
    buf_ref[h:h + ts, :] = u_ref[...].astype(F32)

    t = i * ts + lax.broadcasted_iota(I32, (ts, 1), 0)
    for g, w in enumerate(POOL_WINDOWS):
        cs = slice(g * gdim, (g + 1) * gdim)
        tok = buf_ref[h:h + ts, cs]
        wsum = tok
        for k in range(1, w):
            wsum = wsum + buf_ref[h - k:h - k + ts, cs]
        cnt = jnp.minimum(t + 1, w).astype(F32)
        p = (wsum / cnt - tok).astype(BF16)
        y = jnp.dot(p, pw_ref[g], preferred_element_type=F32) * sc_ref[:, cs]
        o_ref[:, cs] = y.astype(BF16)


def _pool(z, pool_w_b, pool_scale2, *, batch, seq, width, ts=512):
    n = z.shape[0]
    groups = len(POOL_WINDOWS)
    gdim = width // groups
    nt = seq // ts
    hb = ts // MAX_WINDOW
    return pl.pallas_call(
        functools.partial(_pool_kernel, ts=ts, gdim=gdim),
        grid=(batch, nt),
        in_specs=[pl.BlockSpec((ts, width), lambda b, i: (b * nt + i, 0)),
                  pl.BlockSpec((MAX_WINDOW, width),
                               lambda b, i: (jnp.maximum((b * nt + i) * hb - 1, 0), 0)),
                  pl.BlockSpec((groups, gdim, gdim), lambda b, i: (0, 0, 0)),
                  pl.BlockSpec((1, width), lambda b, i: (0, 0))],
        out_specs=pl.BlockSpec((ts, width), lambda b, i: (b * nt + i, 0)),
        out_shape=jax.ShapeDtypeStruct((n, width), BF16),
        scratch_shapes=[pltpu.VMEM((ts + MAX_WINDOW, width), F32)],
        compiler_params=_cparams(("parallel", "parallel")),
        name="pool_mixer",
    )(z, z, pool_w_b, pool_scale2)


def _attn_kernel(bfirst_ref, blast_ref, q_ref, k_ref, ka_ref, v_ref, o_ref,
                 qa_ref, m_ref, l_ref, acc_ref, kmax_ref, *, blk, n_kblk):
    qi = pl.program_id(2)
    base = (pl.program_id(0) * pl.num_programs(1) + pl.program_id(1)) * n_kblk

    @pl.when(qi == 0)
    def _():
        def knorm(c, run):
            kk = k_ref[pl.ds(pl.multiple_of(c * blk, blk), blk), :].astype(F32)
            return jnp.maximum(run, jnp.sum(kk * kk, axis=1, keepdims=True))

        run = lax.fori_loop(0, n_kblk, knorm, jnp.zeros((blk, 1), F32))
        kmax_ref[0] = jnp.max(jnp.sqrt(run))

    qq = q_ref[...].astype(F32)
    qmax = jnp.max(jnp.sqrt(jnp.sum(qq * qq, axis=1, keepdims=True)))
    thr = bfirst_ref[base + Q_HALVES * qi] - SKIP_MARGIN - 2.0 * qmax * kmax_ref[0]

    def count_skippable(j, a):
        return a + jnp.where(blast_ref[base + j] < thr, 1, 0).astype(I32)

    n_skip = lax.fori_loop(0, Q_HALVES * qi, count_skippable, jnp.int32(0))

    lane = lax.broadcasted_iota(I32, (blk, HEAD_DIM), 1)
    ones3 = jnp.where(lane < 3, 1.0, 0.0).astype(BF16)
    for h in range(Q_HALVES):
        qa_ref[h] = jnp.concatenate([q_ref[h * blk:(h + 1) * blk, :], ones3], axis=1)
    m_ref[...] = jnp.full_like(m_ref, -jnp.inf)
    l_ref[...] = jnp.zeros_like(l_ref)
    acc_ref[...] = jnp.zeros_like(acc_ref)
    reps = blk // LANES

    def step(h, ki, masked):
        ks = pl.multiple_of(ki * blk, blk)
        k_aug = jnp.concatenate([k_ref[pl.ds(ks, blk), :], ka_ref[pl.ds(ks, blk), :]], axis=1)
        s = lax.dot_general(qa_ref[h], k_aug, (((1,), (1,)), ((), ())),
                            preferred_element_type=F32)
        if masked:
            r = lax.broadcasted_iota(I32, (blk, blk), 0)
            c = lax.broadcasted_iota(I32, (blk, blk), 1)
            s = jnp.where(c <= r, s, -jnp.inf)
        m_prev = m_ref[h]
        m_next = jnp.maximum(m_prev, jnp.max(s, axis=1, keepdims=True))
        alpha = jnp.exp(m_prev - m_next)
        p = jnp.exp(s - jnp.concatenate([m_next] * reps, axis=1))
        l_ref[h] = alpha * l_ref[h] + jnp.sum(p, axis=1, keepdims=True)
        acc_ref[h] = alpha * acc_ref[h] + jnp.dot(
            p.astype(BF16), v_ref[pl.ds(ks, blk), :], preferred_element_type=F32)
        m_ref[h] = m_next

    def body(kq, carry):
        for d in range(Q_HALVES):
            for h in range(Q_HALVES):
                step(h, Q_HALVES * kq + d, False)
        return carry

    lax.fori_loop(n_skip // Q_HALVES, qi, body, 0)
    for h in range(Q_HALVES):
        for d in range(h):
            step(h, Q_HALVES * qi + d, False)
        step(h, Q_HALVES * qi + h, True)
        o_ref[h * blk:(h + 1) * blk, :] = (acc_ref[h] / l_ref[h]).astype(BF16)


def _attention(bfirst, blast, z, ka, *, batch, seq, n_heads, q_col, k_col, v_col, blk=ATTN_BLOCK):
    n = z.shape[0]
    qb = Q_HALVES * blk
    nq = seq // qb
    grid_spec = pltpu.PrefetchScalarGridSpec(
        num_scalar_prefetch=2,
        grid=(batch, n_heads, nq),
        in_specs=[pl.BlockSpec((qb, HEAD_DIM), lambda b, h, i, bf, bl: (b * nq + i, q_col + h)),
                  pl.BlockSpec((seq, HEAD_DIM), lambda b, h, i, bf, bl: (b, k_col + h)),
                  pl.BlockSpec((seq, HEAD_DIM), lambda b, h, i, bf, bl: (b, h)),
                  pl.BlockSpec((seq, HEAD_DIM), lambda b, h, i, bf, bl: (b, v_col + h))],
        out_specs=pl.BlockSpec((qb, HEAD_DIM), lambda b, h, i, bf, bl: (b * nq + i, h)),
        scratch_shapes=[pltpu.VMEM((Q_HALVES, blk, 2 * HEAD_DIM), BF16),
                        pltpu.VMEM((Q_HALVES, blk, LANES), F32),
                        pltpu.VMEM((Q_HALVES, blk, LANES), F32),
                        pltpu.VMEM((Q_HALVES, blk, HEAD_DIM), F32),
                        pltpu.SMEM((1,), F32)],
    )
    return pl.pallas_call(
        functools.partial(_attn_kernel, blk=blk, n_kblk=seq // blk),
        grid_spec=grid_spec,
        out_shape=jax.ShapeDtypeStruct((n, n_heads * HEAD_DIM), BF16),
        compiler_params=_cparams(("arbitrary", "arbitrary", "arbitrary")),
        name="fox_attention",
    )(bfirst, blast, z, z, ka, z)


def _merge_kernel(x_ref, p_ref, a_ref, wg0_ref, wg1_ref, wbp_ref, wba_ref, o_ref, xb_ref):
    j = pl.program_id(1)

    @pl.when(j == 0)
    def _():
        xb_ref[...] = x_ref[...].astype(BF16)

    xb = xb_ref[...]
    g0 = jax.nn.sigmoid(jnp.dot(xb, wg0_ref[...], preferred_element_type=F32))
    g1 = jax.nn.sigmoid(jnp.dot(xb, wg1_ref[...], preferred_element_type=F32))
    yp = jnp.dot(p_ref[...], wbp_ref[...], preferred_element_type=F32)
    ya = jnp.dot(a_ref[...], wba_ref[...], preferred_element_type=F32)
    o_ref[...] = (g0 * yp + g1 * ya).astype(BF16)


def _merge(x2, pool_o, attn_o, wg, wbp, wba, *, bm=1024, bn=512):
    n, d = x2.shape
    wp = pool_o.shape[1]
    wa = attn_o.shape[1]
    nj = d // bn
    return pl.pallas_call(
        _merge_kernel,
        grid=(n // bm, nj),
        in_specs=[pl.BlockSpec((bm, d), lambda i, j: (i, 0)),
                  pl.BlockSpec((bm, wp), lambda i, j: (i, 0)),
                  pl.BlockSpec((bm, wa), lambda i, j: (i, 0)),
                  pl.BlockSpec((d, bn), lambda i, j: (0, j)),
                  pl.BlockSpec((d, bn), lambda i, j: (0, nj + j)),
                  pl.BlockSpec((wp, bn), lambda i, j: (0, j)),
                  pl.BlockSpec((wa, bn), lambda i, j: (0, j))],
        out_specs=pl.BlockSpec((bm, bn), lambda i, j: (i, j)),
        out_shape=jax.ShapeDtypeStruct((n, d), BF16),
        scratch_shapes=[pltpu.VMEM((bm, d), BF16)],
        compiler_params=_cparams(("parallel", "arbitrary")),
        name="branch_merge",
    )(x2, pool_o, attn_o, wg, wg, wbp, wba)


def _out_ln_router_kernel(x_ref, m_ref, wo_ref, g_ref, b_ref, wrh_ref, wrl_ref,
                          x1_ref, x1b_ref, x1p_ref, lg_ref):
    mix = jnp.dot(m_ref[...], wo_ref[...], preferred_element_type=F32)
    x1 = _layer_norm(ALPHA * x_ref[...] + mix, g_ref[...], b_ref[...])
    x1_ref[...] = x1
    hi = x1.astype(BF16)
    x1b_ref[...] = hi
    half = PACK_TILE // 2
    for t in range(x1.shape[1] // PACK_TILE):
        x1p_ref[:, t * half:(t + 1) * half] = _pack_bf16_pairs(x1[:, t * PACK_TILE:(t + 1) * PACK_TILE])
    lo = (x1 - hi.astype(F32)).astype(BF16)
    lg_ref[...] = (jnp.dot(hi, wrh_ref[...], preferred_element_type=F32)
                   + jnp.dot(lo, wrh_ref[...], preferred_element_type=F32)
                   + jnp.dot(hi, wrl_ref[...], preferred_element_type=F32))


def _out_ln_router(x2, merged, wo, g1, b1, wr_hi, wr_lo, *, bm=512):
    n, d = x2.shape
    row = lambda i: (i, 0)
    fix = lambda i: (0, 0)
    return pl.pallas_call(
        _out_ln_router_kernel,
        grid=(n // bm,),
        in_specs=[pl.BlockSpec((bm, d), row), pl.BlockSpec((bm, d), row),
                  pl.BlockSpec((d, d), fix), pl.BlockSpec((1, d), fix), pl.BlockSpec((1, d), fix),
                  pl.BlockSpec((d, LANES), fix), pl.BlockSpec((d, LANES), fix)],
        out_specs=[pl.BlockSpec((bm, d), row), pl.BlockSpec((bm, d), row),
                   pl.BlockSpec((bm, d // 2), row), pl.BlockSpec((bm, LANES), row)],
        out_shape=[jax.ShapeDtypeStruct((n, d), F32), jax.ShapeDtypeStruct((n, d), BF16),
                   jax.ShapeDtypeStruct((n, d // 2), I32), jax.ShapeDtypeStruct((n, LANES), F32)],
        compiler_params=_cparams(("parallel",)),
        name="outproj_ln1_router",
    )(x2, merged, wo, g1, b1, wr_hi, wr_lo)


def _route_kernel(lg_ref, bias_ref, idx_ref, gate_ref, rank_ref, cnt_ref, carry_ref,
                  *, bm, n_experts):
    i = pl.program_id(0)

    @pl.when(i == 0)
    def _():
        carry_ref[...] = jnp.zeros_like(carry_ref)

    lane = lax.broadcasted_iota(I32, (bm, LANES), 1)
    lane_f = lane.astype(F32)
    scores = jax.nn.sigmoid(lg_ref[...])
    sel = jnp.where(lane < n_experts, scores + bias_ref[...], -jnp.inf)

    hits, gates = [], []
    gsum = jnp.zeros((bm, 1), F32)
    member = jnp.zeros((bm, LANES), F32)
    idx_out = jnp.zeros((bm, LANES), F32)
    for k in range(TOP_K):
        m = jnp.max(sel, axis=1, keepdims=True)
        ik = jnp.min(jnp.where(sel == m, lane_f, float(LANES)), axis=1, keepdims=True)
        hit = lane_f == ik
        gk = jnp.sum(jnp.where(hit, scores, 0.0), axis=1, keepdims=True)
        sel = jnp.where(hit, -jnp.inf, sel)
        member = jnp.where(hit, 1.0, member)
        idx_out = jnp.where(lane == k, ik, idx_out)
        gsum = gsum + gk
        hits.append(hit)
        gates.append(gk)

    row = lax.broadcasted_iota(I32, (bm, bm), 0)
    col = lax.broadcasted_iota(I32, (bm, bm), 1)
    before = jnp.where(col < row, 1.0, 0.0).astype(BF16)
    prefix = jnp.dot(before, member.astype(BF16), preferred_element_type=F32) + carry_ref[...]

    gate_out = jnp.zeros((bm, LANES), F32)
    rank_out = jnp.zeros((bm, LANES), F32)
    for k in range(TOP_K):
        rk = jnp.sum(jnp.where(hits[k], prefix, 0.0), axis=1, keepdims=True)
        rank_out = jnp.where(lane == k, rk, rank_out)
        gate_out = jnp.where(lane == k, gates[k] / gsum * ROUTE_SCALE, gate_out)

    total = carry_ref[...] + jnp.sum(member, axis=0, keepdims=True)
    carry_ref[...] = total
    idx_ref[...] = idx_out.astype(I32)
    gate_ref[...] = gate_out
    rank_ref[...] = rank_out.astype(I32)
    cnt_ref[...] = jnp.broadcast_to(total, cnt_ref.shape).astype(I32)


def _route(logits, bias_pad, *, n_experts, bm=TOK_TILE):
    n = logits.shape[0]
    row = lambda i: (i, 0)
    fix = lambda i: (0, 0)
    return pl.pallas_call(
        functools.partial(_route_kernel, bm=bm, n_experts=n_experts),
        grid=(n // bm,),
        in_specs=[pl.BlockSpec((bm, LANES), row), pl.BlockSpec((1, LANES), fix)],
        out_specs=[pl.BlockSpec((bm, LANES), row), pl.BlockSpec((bm, LANES), row),
                   pl.BlockSpec((bm, LANES), row), pl.BlockSpec((8, LANES), fix)],
        out_shape=[jax.ShapeDtypeStruct((n, LANES), I32), jax.ShapeDtypeStruct((n, LANES), F32),
                   jax.ShapeDtypeStruct((n, LANES), I32), jax.ShapeDtypeStruct((8, LANES), I32)],
        scratch_shapes=[pltpu.VMEM((1, LANES), F32)],
        compiler_params=_cparams(("arbitrary",)),
        name="route_topk",
    )(logits, bias_pad)


def _dispatch_kernel(zs_ref, dest_ref, x_ref, xs_hbm, zero_ref, sem, zsem, *, bt, n_experts):
    i = pl.program_id(0)

    def zero_copy(e):
        zs = pl.multiple_of(zs_ref[e], SUB_ROWS)
        return pltpu.make_async_copy(zero_ref, xs_hbm.at[pl.ds(zs, SUB_ROWS), :], zsem)

    @pl.when(i == 0)
    def _():
        zero_ref[...] = jnp.zeros_like(zero_ref)

        def zstart(e, c):
            zero_copy(e).start()
            return c

        def zwait(e, c):
            zero_copy(e).wait()
            return c

        lax.fori_loop(0, n_experts, zstart, 0)
        lax.fori_loop(0, n_experts, zwait, 0)

    def row_copy(t, k):
        dst = dest_ref[0, 0, t * TOP_K + k]
        return pltpu.make_async_copy(x_ref.at[pl.ds(t, 1), :], xs_hbm.at[pl.ds(dst, 1), :], sem)

    def start(t, c):
        for k in range(TOP_K):
            row_copy(t, k).start(priority=k % 2)
        return c

    def wait(t, c):
        for k in range(TOP_K):
            row_copy(t, k).wait()
        return c

    lax.fori_loop(0, bt, start, 0)
    lax.fori_loop(0, bt, wait, 0)


def _dispatch(zstart, dest3, x1p, *, rows_total, n_experts, bt=TOK_TILE):
    n, d = x1p.shape
    grid_spec = pltpu.PrefetchScalarGridSpec(
        num_scalar_prefetch=1,
        grid=(n // bt,),
        in_specs=[pl.BlockSpec((1, 1, bt * TOP_K), lambda i, zs: (i, 0, 0),
                               memory_space=pltpu.SMEM),
                  pl.BlockSpec((bt, d), lambda i, zs: (i, 0))],
        out_specs=pl.BlockSpec(memory_space=pl.ANY),
        scratch_shapes=[pltpu.VMEM((SUB_ROWS, d), I32),
                        pltpu.SemaphoreType.DMA(()),
                        pltpu.SemaphoreType.DMA(())],
    )
    return pl.pallas_call(
        functools.partial(_dispatch_kernel, bt=bt, n_experts=n_experts),
        grid_spec=grid_spec,
        out_shape=jax.ShapeDtypeStruct((rows_total, d), I32),
        compiler_params=_cparams(("arbitrary",)),
        name="dispatch_rows",
    )(zstart, dest3, x1p)


def _experts_kernel(ce_ref, cv_ref, tot_ref, x_ref, w1_ref, w3_ref, w2_ref, o_ref,
                    w13_ref, acc_ref, h_ref, w2b_ref, *, n_up, fdim):
    c = pl.program_id(0)
    j = pl.program_id(1)
    nv = cv_ref[c]
    del ce_ref, tot_ref
    all_rows = slice(0, ROW_CHUNK)
    sub_blocks = [slice(sb * SUB_ROWS, (sb + 1) * SUB_ROWS) for sb in range(ROW_CHUNK // SUB_ROWS)]

    def up_rows(rows, first, last):
        xb = _unpack_bf16_pairs(x_ref[rows, :]).astype(BF16)
        part = jnp.dot(xb, w13_ref[...], preferred_element_type=F32)
        if not first:
            part = part + acc_ref[rows, :]
        if last:
            h1 = part[:, :fdim]
            h_ref[rows, :] = (h1 * jax.nn.sigmoid(h1) * part[:, fdim:]).astype(BF16)
        else:
            acc_ref[rows, :] = part

    def cast_up_weights():
        w13_ref[:, :fdim] = w1_ref[...].astype(BF16)
        w13_ref[:, fdim:] = w3_ref[...].astype(BF16)

    def up(first, last):
        @pl.when(nv == ROW_CHUNK)
        def _():
            cast_up_weights()
            up_rows(all_rows, first, last)

        @pl.when(nv < ROW_CHUNK)
        def _():
            cast_up_weights()
            for rows in sub_blocks:
                @pl.when(rows.start < nv)
                def _():
                    up_rows(rows, first, last)

    @pl.when(j == 0)
    def _():
        up(True, n_up == 1)

    if n_up > 2:
        @pl.when((j > 0) & (j < n_up - 1))
        def _():
            up(False, False)

    if n_up > 1:
        @pl.when(j == n_up - 1)
        def _():
            up(False, True)

    @pl.when(j >= n_up)
    def _():
        @pl.when(nv == ROW_CHUNK)
        def _():
            w2b_ref[...] = w2_ref[...].astype(BF16)
            o_ref[...] = _pack_bf16_pairs(
                jnp.dot(h_ref[...], w2b_ref[...], preferred_element_type=F32))

        @pl.when(nv < ROW_CHUNK)
        def _():
            w2b_ref[...] = w2_ref[...].astype(BF16)
            for rows in sub_blocks:
                @pl.when(rows.start < nv)
                def _():
                    o_ref[rows, :] = _pack_bf16_pairs(
                        jnp.dot(h_ref[rows, :], w2b_ref[...], preferred_element_type=F32))

                @pl.when(rows.start >= nv)
                def _():
                    o_ref[rows, :] = jnp.zeros((SUB_ROWS, o_ref.shape[1]), I32)


def _experts(chunk_expert, chunk_valid, total, xs, w1, w3, w2, *, n_chunks):
    d = w1.shape[1]
    fdim = w1.shape[2]
    n_up = d // UP_K_TILE
    n_down = d // DOWN_N_TILE

    def x_map(c, j, ce, cv, tot):
        return (jnp.minimum(c, tot[0] - 1), jnp.minimum(j, n_up - 1))

    def w13_map(c, j, ce, cv, tot):
        return (ce[c], jnp.minimum(j, n_up - 1), 0)

    def w2_map(c, j, ce, cv, tot):
        return (ce[c], 0, jnp.maximum(j - n_up, 0))

    def o_map(c, j, ce, cv, tot):
        return (c, jnp.maximum(j - n_up, 0))

    grid_spec = pltpu.PrefetchScalarGridSpec(
        num_scalar_prefetch=3,
        grid=(n_chunks, n_up + n_down),
        in_specs=[pl.BlockSpec((ROW_CHUNK, UP_K_TILE // 2), x_map),
                  pl.BlockSpec((None, UP_K_TILE, fdim), w13_map),
                  pl.BlockSpec((None, UP_K_TILE, fdim), w13_map),
                  pl.BlockSpec((None, fdim, DOWN_N_TILE), w2_map)],
        out_specs=pl.BlockSpec((ROW_CHUNK, DOWN_N_TILE // 2), o_map),
        scratch_shapes=[pltpu.VMEM((UP_K_TILE, 2 * fdim), BF16),
                        pltpu.VMEM((ROW_CHUNK, 2 * fdim), F32),
                        pltpu.VMEM((ROW_CHUNK, fdim), BF16),
                        pltpu.VMEM((fdim, DOWN_N_TILE), BF16)],
    )
    return pl.pallas_call(
        functools.partial(_experts_kernel, n_up=n_up, fdim=fdim),
        grid_spec=grid_spec,
        out_shape=jax.ShapeDtypeStruct((n_chunks * ROW_CHUNK, d // 2), I32),
        compiler_params=_cparams(("arbitrary", "arbitrary")),
        name="routed_experts",
    )(chunk_expert, chunk_valid, total, xs, w1, w3, w2)


def _shared_kernel(x_ref, w13_ref, w2_ref, o_ref, *, tf):
    f = pl.program_id(1)
    r = jnp.dot(x_ref[...], w13_ref[...], preferred_element_type=F32)
    h1 = r[:, :tf]
    h = (h1 * jax.nn.sigmoid(h1) * r[:, tf:]).astype(BF16)
    y = jnp.dot(h, w2_ref[...], preferred_element_type=F32)

    @pl.when(f == 0)
    def _():
        o_ref[...] = y

    @pl.when(f > 0)
    def _():
        o_ref[...] += y


def _shared_expert(x1b, w13s, w2s, *, bm=1024, tf=SHARED_F_TILE):
    n, d = x1b.shape
    nf = w2s.shape[0] // tf
    return pl.pallas_call(
        functools.partial(_shared_kernel, tf=tf),
        grid=(n // bm, nf),
        in_specs=[pl.BlockSpec((bm, d), lambda i, f: (i, 0)),
                  pl.BlockSpec((d, 2 * tf), lambda i, f: (0, f)),
                  pl.BlockSpec((tf, d), lambda i, f: (f, 0))],
        out_specs=pl.BlockSpec((bm, d), lambda i, f: (i, 0)),
        out_shape=jax.ShapeDtypeStruct((n, d), F32),
        compiler_params=_cparams(("parallel", "arbitrary")),
        name="shared_expert",
    )(x1b, w13s, w2s)


def _combine_kernel(dcur_ref, dnext_ref, x1_ref, ysh_ref, gate_ref, g_ref, b_ref, ys_hbm, o_ref,
                    buf_ref, sem, *, bt):
    i = pl.program_id(0)
    slot = lax.rem(i, 2)

    def row_copy(dref, s, t, k):
        src = dref[0, 0, t * TOP_K + k]
        return pltpu.make_async_copy(ys_hbm.at[pl.ds(src, 1), :],
                                     buf_ref.at[s, k, pl.ds(t, 1), :], sem.at[s])

    def issue(dref, s):
        def start(t, c):
            for k in range(TOP_K):
                row_copy(dref, s, t, k).start(priority=k % 2)
            return c

        lax.fori_loop(0, bt, start, 0)

    @pl.when(i == 0)
    def _():
        issue(dcur_ref, slot)

    @pl.when(i + 1 < pl.num_programs(0))
    def _():
        issue(dnext_ref, 1 - slot)

    acc = ALPHA * x1_ref[...] + ysh_ref[...]

    def wait(t, c):
        for k in range(TOP_K):
            row_copy(dcur_ref, slot, t, k).wait()
        return c

    lax.fori_loop(0, bt, wait, 0)
    gate = gate_ref[...]
    half = PACK_TILE // 2
    for k in range(TOP_K):
        yk = jnp.concatenate([_unpack_bf16_pairs(buf_ref[slot, k, :, t * half:(t + 1) * half])
                              for t in range(acc.shape[1] // PACK_TILE)], axis=1)
        acc = acc + gate[:, k:k + 1] * yk
    o_ref[...] = _layer_norm(acc, g_ref[...], b_ref[...])


def _combine(dest3, x1, ysh, gate, g2, b2, ys, *, bt=TOK_TILE):
    n, d = x1.shape
    nt = n // bt
    row = lambda i: (i, 0)
    fix = lambda i: (0, 0)
    return pl.pallas_call(
        functools.partial(_combine_kernel, bt=bt),
        grid=(nt,),
        in_specs=[pl.BlockSpec((1, 1, bt * TOP_K), lambda i: (i, 0, 0), memory_space=pltpu.SMEM),
                  pl.BlockSpec((1, 1, bt * TOP_K), lambda i: (jnp.minimum(i + 1, nt - 1), 0, 0),
                               memory_space=pltpu.SMEM),
                  pl.BlockSpec((bt, d), row), pl.BlockSpec((bt, d), row),
                  pl.BlockSpec((bt, LANES), row),
                  pl.BlockSpec((1, d), fix), pl.BlockSpec((1, d), fix),
                  pl.BlockSpec(memory_space=pl.ANY)],
        out_specs=pl.BlockSpec((bt, d), row),
        out_shape=jax.ShapeDtypeStruct((n, d), F32),
        scratch_shapes=[pltpu.VMEM((2, TOP_K, bt, d // 2), I32), pltpu.SemaphoreType.DMA((2,))],
        compiler_params=_cparams(("arbitrary",)),
        name="combine_ln2",
    )(dest3, dest3, x1, ysh, gate, g2, b2, ys)


def _layer(x, w_in, b_forget, pool_w, pool_scale, w_branch_pool, w_branch_attn, w_out,
           ln1_g, ln1_b, w_router, router_bias, w1, w3, w2, w_shared1, w_shared3, w_shared2,
           ln2_g, ln2_b):
    batch, seq, d = x.shape
    n = batch * seq
    pool_width = w_branch_pool.shape[0]
    attn_width = w_branch_attn.shape[0]
    n_heads = attn_width // HEAD_DIM
    n_experts = w1.shape[0]
    off_q = pool_width
    off_k = off_q + attn_width
    off_v = off_k + attn_width
    off_f = off_v + attn_width
    off_gate = off_f + n_heads

    x2 = x.reshape(n, d)

    w_main = w_in[:, :off_f].astype(BF16)
    wf_pad = jnp.pad(w_in[:, off_f:off_gate], ((0, 0), (0, LANES - n_heads))).astype(BF16)
    bf_pad = jnp.pad(b_forget.astype(F32), (0, LANES - n_heads)).reshape(1, LANES)
    w_gate = w_in[:, off_gate:].astype(BF16)
    piece = jnp.arange(3)[:, None, None]
    src = jnp.arange(LANES)[None, :, None]
    dst = jnp.arange(n_heads * LANES)[None, None, :]
    sel = ((src < n_heads) & (dst == src * LANES + piece)).astype(BF16)
    wr = jnp.pad(w_router.astype(F32), ((0, 0), (0, LANES - n_experts)))
    wr_hi = wr.astype(BF16)
    wr_lo = (wr - wr_hi.astype(F32)).astype(BF16)
    rb_pad = jnp.pad(router_bias.astype(F32), (0, LANES - n_experts)).reshape(1, LANES)
    tf = SHARED_F_TILE
    fs = w_shared1.shape[1]
    w13s = jnp.concatenate([w_shared1.reshape(d, fs // tf, tf), w_shared3.reshape(d, fs // tf, tf)],
                           axis=2).reshape(d, 2 * fs).astype(BF16)

    z = _inproj(x2, w_main, q_lo=off_q, q_hi=off_k, q_scale=HEAD_DIM ** -0.5)
    ka, edge = _forget_bias(x2, wf_pad, bf_pad, sel, batch=batch, seq=seq, n_heads=n_heads)
    n_kblk = seq // ATTN_BLOCK
    edge = edge.reshape(batch, n_kblk, 8, LANES)[:, :, :2, :n_heads]
    bfirst = edge[:, :, 0, :].transpose(0, 2, 1).reshape(-1)
    blast = edge[:, :, 1, :].transpose(0, 2, 1).reshape(-1)
    pool_o = _pool(z, pool_w.astype(BF16), pool_scale.astype(F32).reshape(1, pool_width),
                   batch=batch, seq=seq, width=pool_width)
    attn_o = _attention(bfirst, blast, z, ka, batch=batch, seq=seq, n_heads=n_heads,
                        q_col=off_q // HEAD_DIM, k_col=off_k // HEAD_DIM, v_col=off_v // HEAD_DIM)
    merged = _merge(x2, pool_o, attn_o, w_gate, w_branch_pool.astype(BF16),
                    w_branch_attn.astype(BF16))
    x1, x1b, x1p, logits = _out_ln_router(x2, merged, w_out.astype(BF16),
                                     ln1_g.astype(F32).reshape(1, d), ln1_b.astype(F32).reshape(1, d),
                                     wr_hi, wr_lo)

    idx, gate, rank, cnt = _route(logits, rb_pad, n_experts=n_experts)
    counts = cnt[0, :n_experts]
    n_assign = n * TOP_K
    n_chunks = -(-n_assign // ROW_CHUNK) + n_experts
    chunks_per = (counts + ROW_CHUNK - 1) // ROW_CHUNK
    chunk_end = jnp.cumsum(chunks_per)
    chunk_begin = chunk_end - chunks_per
    total = chunk_end[-1]
    group_start = chunk_begin * ROW_CHUNK
    cids = jnp.arange(n_chunks, dtype=I32)
    c_eff = jnp.minimum(cids, total - 1)
    chunk_expert = jnp.minimum(jnp.sum((chunk_end[None, :] <= c_eff[:, None]).astype(I32), axis=1),
                               n_experts - 1).astype(I32)
    chunk_valid = jnp.where(cids < total,
                            jnp.minimum(counts[chunk_expert] - (cids - chunk_begin[chunk_expert]) * ROW_CHUNK,
                                        ROW_CHUNK), 0).astype(I32)
    idx_k = idx[:, :TOP_K]
    dest = group_start[idx_k].astype(I32) + rank[:, :TOP_K]
    dest3 = dest.reshape(n // TOK_TILE, 1, TOK_TILE * TOP_K)
    zstart = ((group_start + counts) // SUB_ROWS * SUB_ROWS).astype(I32)
    rows_total = (n_chunks + 1) * ROW_CHUNK

    xs = _dispatch(zstart, dest3, x1p, rows_total=rows_total, n_experts=n_experts)
    ys = _experts(chunk_expert, chunk_valid, total.reshape(1).astype(I32), xs, w1, w3, w2,
                  n_chunks=n_chunks)
    ysh = _shared_expert(x1b, w13s, w_shared2.astype(BF16))
    out = _combine(dest3, x1, ysh, gate, ln2_g.astype(F32).reshape(1, d),
                   ln2_b.astype(F32).reshape(1, d), ys)
    return out.reshape(batch, seq, d)


def kernel(x, w_in, b_forget, pool_w, pool_scale, w_branch_pool, w_branch_attn, w_out, ln1_g, ln1_b,
           w_router, router_bias, w1, w3, w2, w_shared1, w_shared3, w_shared2, ln2_g, ln2_b):
    for l in range(w_in.shape[0]):
        x = _layer(x, w_in[l], b_forget[l], pool_w[l], pool_scale[l], w_branch_pool[l],
                   w_branch_attn[l], w_out[l], ln1_g[l], ln1_b[l], w_router[l], router_bias[l],
                   w1[l], w3[l], w2[l], w_shared1[l], w_shared3[l], w_shared2[l], ln2_g[l], ln2_b[l])
    return x
```

```python
import functools

import jax
import jax.numpy as jnp
from jax import lax
from jax.experimental import pallas as pl
from jax.experimental.pallas import tpu as pltpu

F32 = jnp.float32
BF16 = jnp.bfloat16
I32 = jnp.int32

POOL_WINDOWS = (2, 4, 8, 16)
HEAD_DIM = 128
TOP_K = 6
ROUTE_SCALE = 1.0
DEPTH = 1
ALPHA = (2 * DEPTH) ** 0.25
LN_EPS = 1e-5

LANES = 128
MAX_WINDOW = max(POOL_WINDOWS)
ATTN_BLOCK = 512
Q_HALVES = 2
SKIP_MARGIN = 110.0
ROW_CHUNK = 1024
SUB_ROWS = 256
PACK_TILE = 512
UP_K_TILE = PACK_TILE
DOWN_N_TILE = PACK_TILE
SHARED_F_TILE = 256
TOK_TILE = 256
ROW_UNROLL = 8
ROW_GROUPS = 2
VMEM_LIMIT = 56 * 1024 * 1024


def _cparams(sem, vmem=VMEM_LIMIT):
    return pltpu.CompilerParams(dimension_semantics=sem, vmem_limit_bytes=vmem)


def _split3(v):
    hi = v.astype(BF16)
    r1 = v - hi.astype(F32)
    mid = r1.astype(BF16)
    lo = (r1 - mid.astype(F32)).astype(BF16)
    return hi, mid, lo


def _pack_bf16_pairs(v):
    c = v.shape[1] // 2
    lo = lax.bitcast_convert_type(v[:, :c].astype(BF16).astype(F32), I32)
    hi = lax.bitcast_convert_type(v[:, c:].astype(BF16).astype(F32), I32)
    return ((lo >> 16) & 0xFFFF) | (hi & -65536)


def _unpack_bf16_pairs(w):
    lo = lax.bitcast_convert_type(w << 16, F32)
    hi = lax.bitcast_convert_type(w & -65536, F32)
    return jnp.concatenate([lo, hi], axis=1)


def _layer_norm(v, g, b):
    mu = jnp.mean(v, axis=-1, keepdims=True)
    d = v - mu
    var = jnp.mean(d * d, axis=-1, keepdims=True)
    return d * lax.rsqrt(var + LN_EPS) * g + b


def _inproj_kernel(x_ref, w_ref, o_ref, xb_ref, *, bn, q_lo, q_hi, q_scale):
    j = pl.program_id(1)

    @pl.when(j == 0)
    def _():
        xb_ref[...] = x_ref[...].astype(BF16)

    acc = jnp.dot(xb_ref[...], w_ref[...], preferred_element_type=F32)
    col0 = j * bn
    s = jnp.where((col0 >= q_lo) & (col0 < q_hi), q_scale, 1.0).astype(F32)
    o_ref[...] = (acc * s).astype(BF16)


def _inproj(x2, w_main, *, q_lo, q_hi, q_scale, bm=1024, bn=512):
    n, d = x2.shape
    c = w_main.shape[1]
    return pl.pallas_call(
        functools.partial(_inproj_kernel, bn=bn, q_lo=q_lo, q_hi=q_hi, q_scale=q_scale),
        grid=(n // bm, c // bn),
        in_specs=[pl.BlockSpec((bm, d), lambda i, j: (i, 0)),
                  pl.BlockSpec((d, bn), lambda i, j: (0, j))],
        out_specs=pl.BlockSpec((bm, bn), lambda i, j: (i, j)),
        out_shape=jax.ShapeDtypeStruct((n, c), BF16),
        scratch_shapes=[pltpu.VMEM((bm, d), BF16)],
        compiler_params=_cparams(("parallel", "arbitrary")),
        name="inproj",
    )(x2, w_main)


def _forget_kernel(x_ref, wf_ref, b_ref, sel_ref, ka_ref, edge_ref, carry_ref, *, ts, n_heads):
    i = pl.program_id(1)

    @pl.when(i == 0)
    def _():
        carry_ref[...] = jnp.zeros_like(carry_ref)

    z = jnp.dot(x_ref[...].astype(BF16), wf_ref[...], preferred_element_type=F32) + b_ref[...]
    lf = jnp.minimum(z, 0.0) - jnp.log1p(jnp.exp(-jnp.abs(z)))
    lane = lax.broadcasted_iota(I32, lf.shape, 1)
    lf = jnp.where(lane < n_heads, lf, 0.0)

    row = lax.broadcasted_iota(I32, (ts, ts), 0)
    col = lax.broadcasted_iota(I32, (ts, ts), 1)
    tri = jnp.where(col <= row, 1.0, 0.0).astype(BF16)
    hi, mid, lo = _split3(lf)
    cs = (jnp.dot(tri, hi, preferred_element_type=F32)
          + jnp.dot(tri, mid, preferred_element_type=F32)
          + jnp.dot(tri, lo, preferred_element_type=F32)) + carry_ref[...]
    carry_ref[...] = cs[ts - 1:ts, :]
    sub = lax.broadcasted_iota(I32, (8, LANES), 0)
    edge_ref[0] = jnp.where(sub == 0, -cs[0:1, :], -cs[ts - 1:ts, :])

    h2, m2, l2 = _split3(-cs)
    ka = (jnp.dot(h2, sel_ref[0], preferred_element_type=F32)
          + jnp.dot(m2, sel_ref[1], preferred_element_type=F32)
          + jnp.dot(l2, sel_ref[2], preferred_element_type=F32))
    ka_ref[...] = ka.astype(BF16)


def _forget_bias(x2, wf_pad, b_pad, sel, *, batch, seq, n_heads, ts=ATTN_BLOCK):
    n, d = x2.shape
    nt = seq // ts
    return pl.pallas_call(
        functools.partial(_forget_kernel, ts=ts, n_heads=n_heads),
        grid=(batch, nt),
        in_specs=[pl.BlockSpec((ts, d), lambda b, i: (b * nt + i, 0)),
                  pl.BlockSpec((d, LANES), lambda b, i: (0, 0)),
                  pl.BlockSpec((1, LANES), lambda b, i: (0, 0)),
                  pl.BlockSpec((3, LANES, n_heads * LANES), lambda b, i: (0, 0, 0))],
        out_specs=[pl.BlockSpec((ts, n_heads * LANES), lambda b, i: (b * nt + i, 0)),
                   pl.BlockSpec((1, 8, LANES), lambda b, i: (b * nt + i, 0, 0))],
        out_shape=[jax.ShapeDtypeStruct((n, n_heads * LANES), BF16),
                   jax.ShapeDtypeStruct((batch * nt, 8, LANES), F32)],
        scratch_shapes=[pltpu.VMEM((1, LANES), F32)],
        compiler_params=_cparams(("arbitrary", "arbitrary")),
        name="forget_bias",
    )(x2, wf_pad, b_pad, sel)


def _pool_kernel(u_ref, halo_ref, pw_ref, sc_ref, o_ref, buf_ref, *, ts, gdim):
    i = pl.program_id(1)
    h = MAX_WINDOW
    halo = halo_ref[...].astype(F32)
    buf_ref[0:h, :] = jnp.where(i > 0, halo, 0.0)
    buf_ref[h:h + ts, :] = u_ref[...].astype(F32)

    t = i * ts + lax.broadcasted_iota(I32, (ts, 1), 0)
    for g, w in enumerate(POOL_WINDOWS):
        cs = slice(g * gdim, (g + 1) * gdim)
        tok = buf_ref[h:h + ts, cs]
        wsum = tok
        for k in range(1, w):
            wsum = wsum + buf_ref[h - k:h - k + ts, cs]
        cnt = jnp.minimum(t + 1, w).astype(F32)
        p = (wsum / cnt - tok).astype(BF16)
        y = jnp.dot(p, pw_ref[g], preferred_element_type=F32) * sc_ref[:, cs]
        o_ref[:, cs] = y.astype(BF16)


def _pool(z, pool_w_b, pool_scale2, *, batch, seq, width, ts=512):
    n = z.shape[0]
    groups = len(POOL_WINDOWS)
    gdim = width // groups
    nt = seq // ts
    hb = ts // MAX_WINDOW
    return pl.pallas_call(
        functools.partial(_pool_kernel, ts=ts, gdim=gdim),
        grid=(batch, nt),
        in_specs=[pl.BlockSpec((ts, width), lambda b, i: (b * nt + i, 0)),
                  pl.BlockSpec((MAX_WINDOW, width),
                               lambda b, i: (jnp.maximum((b * nt + i) * hb - 1, 0), 0)),
                  pl.BlockSpec((groups, gdim, gdim), lambda b, i: (0, 0, 0)),
                  pl.BlockSpec((1, width), lambda b, i: (0, 0))],
        out_specs=pl.BlockSpec((ts, width), lambda b, i: (b * nt + i, 0)),
        out_shape=jax.ShapeDtypeStruct((n, width), BF16),
        scratch_shapes=[pltpu.VMEM((ts + MAX_WINDOW, width), F32)],
        compiler_params=_cparams(("parallel", "parallel")),
        name="pool_mixer",
    )(z, z, pool_w_b, pool_scale2)


def _attn_kernel(bfirst_ref, blast_ref, q_ref, k_ref, ka_ref, v_ref, o_ref,
                 qa_ref, m_ref, l_ref, acc_ref, kmax_ref, *, blk, n_kblk):
    qi = pl.program_id(2)
    base = (pl.program_id(0) * pl.num_programs(1) + pl.program_id(1)) * n_kblk

    @pl.when(qi == 0)
    def _():
        def knorm(c, run):
            kk = k_ref[pl.ds(pl.multiple_of(c * blk, blk), blk), :].astype(F32)
            return jnp.maximum(run, jnp.sum(kk * kk, axis=1, keepdims=True))

        run = lax.fori_loop(0, n_kblk, knorm, jnp.zeros((blk, 1), F32))
        kmax_ref[0] = jnp.max(jnp.sqrt(run))

    qq = q_ref[...].astype(F32)
    qmax = jnp.max(jnp.sqrt(jnp.sum(qq * qq, axis=1, keepdims=True)))
    thr = bfirst_ref[base + Q_HALVES * qi] - SKIP_MARGIN - 2.0 * qmax * kmax_ref[0]

    def count_skippable(j, a):
        return a + jnp.where(blast_ref[base + j] < thr, 1, 0).astype(I32)

    n_skip = lax.fori_loop(0, Q_HALVES * qi, count_skippable, jnp.int32(0))

    lane = lax.broadcasted_iota(I32, (blk, HEAD_DIM), 1)
    ones3 = jnp.where(lane < 3, 1.0, 0.0).astype(BF16)
    for h in range(Q_HALVES):
        qa_ref[h] = jnp.concatenate([q_ref[h * blk:(h + 1) * blk, :], ones3], axis=1)
    m_ref[...] = jnp.full_like(m_ref, -jnp.inf)
    l_ref[...] = jnp.zeros_like(l_ref)
    acc_ref[...] = jnp.zeros_like(acc_ref)
    reps = blk // LANES

    def step(h, ki, masked):
        ks = pl.multiple_of(ki * blk, blk)
        k_aug = jnp.concatenate([k_ref[pl.ds(ks, blk), :], ka_ref[pl.ds(ks, blk), :]], axis=1)
        s = lax.dot_general(qa_ref[h], k_aug, (((1,), (1,)), ((), ())),
                            preferred_element_type=F32)
        if masked:
            r = lax.broadcasted_iota(I32, (blk, blk), 0)
            c = lax.broadcasted_iota(I32, (blk, blk), 1)
            s = jnp.where(c <= r, s, -jnp.inf)
        m_prev = m_ref[h]
        m_next = jnp.maximum(m_prev, jnp.max(s, axis=1, keepdims=True))
        alpha = jnp.exp(m_prev - m_next)
        p = jnp.exp(s - jnp.concatenate([m_next] * reps, axis=1))
        l_ref[h] = alpha * l_ref[h] + jnp.sum(p, axis=1, keepdims=True)
        acc_ref[h] = alpha * acc_ref[h] + jnp.dot(
            p.astype(BF16), v_ref[pl.ds(ks, blk), :], preferred_element_type=F32)
        m_ref[h] = m_next

    def body(kq, carry):
        for d in range(Q_HALVES):
            for h in range(Q_HALVES):
                step(h, Q_HALVES * kq + d, False)
        return carry

    lax.fori_loop(n_skip // Q_HALVES, qi, body, 0)
    for h in range(Q_HALVES):
        for d in range(h):
            step(h, Q_HALVES * qi + d, False)
        step(h, Q_HALVES * qi + h, True)
        o_ref[h * blk:(h + 1) * blk, :] = (acc_ref[h] / l_ref[h]).astype(BF16)


def _attention(bfirst, blast, z, ka, *, batch, seq, n_heads, q_col, k_col, v_col, blk=ATTN_BLOCK):
    n = z.shape[0]
    qb = Q_HALVES * blk
    nq = seq // qb
    grid_spec = pltpu.PrefetchScalarGridSpec(
        num_scalar_prefetch=2,
        grid=(batch, n_heads, nq),
        in_specs=[pl.BlockSpec((qb, HEAD_DIM), lambda b, h, i, bf, bl: (b * nq + i, q_col + h)),
                  pl.BlockSpec((seq, HEAD_DIM), lambda b, h, i, bf, bl: (b, k_col + h)),
                  pl.BlockSpec((seq, HEAD_DIM), lambda b, h, i, bf, bl: (b, h)),
                  pl.BlockSpec((seq, HEAD_DIM), lambda b, h, i, bf, bl: (b, v_col + h))],
        out_specs=pl.BlockSpec((qb, HEAD_DIM), lambda b, h, i, bf, bl: (b * nq + i, h)),
        scratch_shapes=[pltpu.VMEM((Q_HALVES, blk, 2 * HEAD_DIM), BF16),
                        pltpu.VMEM((Q_HALVES, blk, LANES), F32),
                        pltpu.VMEM((Q_HALVES, blk, LANES), F32),
                        pltpu.VMEM((Q_HALVES, blk, HEAD_DIM), F32),
                        pltpu.SMEM((1,), F32)],
    )
    return pl.pallas_call(
        functools.partial(_attn_kernel, blk=blk, n_kblk=seq // blk),
        grid_spec=grid_spec,
        out_shape=jax.ShapeDtypeStruct((n, n_heads * HEAD_DIM), BF16),
        compiler_params=_cparams(("arbitrary", "arbitrary", "arbitrary")),
        name="fox_attention",
    )(bfirst, blast, z, z, ka, z)


def _merge_kernel(x_ref, p_ref, a_ref, wg0_ref, wg1_ref, wbp_ref, wba_ref, o_ref, xb_ref):
    j = pl.program_id(1)

    @pl.when(j == 0)
    def _():
        xb_ref[...] = x_ref[...].astype(BF16)

    xb = xb_ref[...]
    g0 = jax.nn.sigmoid(jnp.dot(xb, wg0_ref[...], preferred_element_type=F32))
    g1 = jax.nn.sigmoid(jnp.dot(xb, wg1_ref[...], preferred_element_type=F32))
    yp = jnp.dot(p_ref[...], wbp_ref[...], preferred_element_type=F32)
    ya = jnp.dot(a_ref[...], wba_ref[...], preferred_element_type=F32)
    o_ref[...] = (g0 * yp + g1 * ya).astype(BF16)


def _merge(x2, pool_o, attn_o, wg, wbp, wba, *, bm=1024, bn=512):
    n, d = x2.shape
    wp = pool_o.shape[1]
    wa = attn_o.shape[1]
    nj = d // bn
    return pl.pallas_call(
        _merge_kernel,
        grid=(n // bm, nj),
        in_specs=[pl.BlockSpec((bm, d), lambda i, j: (i, 0)),
                  pl.BlockSpec((bm, wp), lambda i, j: (i, 0)),
                  pl.BlockSpec((bm, wa), lambda i, j: (i, 0)),
                  pl.BlockSpec((d, bn), lambda i, j: (0, j)),
                  pl.BlockSpec((d, bn), lambda i, j: (0, nj + j)),
                  pl.BlockSpec((wp, bn), lambda i, j: (0, j)),
                  pl.BlockSpec((wa, bn), lambda i, j: (0, j))],
        out_specs=pl.BlockSpec((bm, bn), lambda i, j: (i, j)),
        out_shape=jax.ShapeDtypeStruct((n, d), BF16),
        scratch_shapes=[pltpu.VMEM((bm, d), BF16)],
        compiler_params=_cparams(("parallel", "arbitrary")),
        name="branch_merge",
    )(x2, pool_o, attn_o, wg, wg, wbp, wba)


def _out_ln_router_kernel(x_ref, m_ref, wo_ref, g_ref, b_ref, wrh_ref, wrl_ref,
                          x1_ref, x1b_ref, x1p_ref, lg_ref):
    bm = x_ref.shape[0]
    rows_per = bm // ROW_GROUPS
    half = PACK_TILE // 2
    for r in range(ROW_GROUPS):
        rows = slice(r * rows_per, (r + 1) * rows_per)
        mix = jnp.dot(m_ref[rows, :], wo_ref[...], preferred_element_type=F32)
        x1 = _layer_norm(ALPHA * x_ref[rows, :] + mix, g_ref[...], b_ref[...])
        x1_ref[rows, :] = x1
        hi = x1.astype(BF16)
        x1b_ref[rows, :] = hi
        for t in range(x1.shape[1] // PACK_TILE):
            x1p_ref[rows, t * half:(t + 1) * half] = _pack_bf16_pairs(
                x1[:, t * PACK_TILE:(t + 1) * PACK_TILE])
        lo = (x1 - hi.astype(F32)).astype(BF16)
        lg_ref[rows, :] = (jnp.dot(hi, wrh_ref[...], preferred_element_type=F32)
                           + jnp.dot(lo, wrh_ref[...], preferred_element_type=F32)
                           + jnp.dot(hi, wrl_ref[...], preferred_element_type=F32))


def _out_ln_router(x2, merged, wo, g1, b1, wr_hi, wr_lo, *, bm=512):
    n, d = x2.shape
    row = lambda i: (i, 0)
    fix = lambda i: (0, 0)
    return pl.pallas_call(
        _out_ln_router_kernel,
        grid=(n // bm,),
        in_specs=[pl.BlockSpec((bm, d), row), pl.BlockSpec((bm, d), row),
                  pl.BlockSpec((d, d), fix), pl.BlockSpec((1, d), fix), pl.BlockSpec((1, d), fix),
                  pl.BlockSpec((d, LANES), fix), pl.BlockSpec((d, LANES), fix)],
        out_specs=[pl.BlockSpec((bm, d), row), pl.BlockSpec((bm, d), row),
                   pl.BlockSpec((bm, d // 2), row), pl.BlockSpec((bm, LANES), row)],
        out_shape=[jax.ShapeDtypeStruct((n, d), F32), jax.ShapeDtypeStruct((n, d), BF16),
                   jax.ShapeDtypeStruct((n, d // 2), I32), jax.ShapeDtypeStruct((n, LANES), F32)],
        compiler_params=_cparams(("parallel",)),
        name="outproj_ln1_router",
    )(x2, merged, wo, g1, b1, wr_hi, wr_lo)


def _route_kernel(lg_ref, bias_ref, idx_ref, gate_ref, rank_ref, cnt_ref, carry_ref,
                  *, bm, n_experts):
    i = pl.program_id(0)

    @pl.when(i == 0)
    def _():
        carry_ref[...] = jnp.zeros_like(carry_ref)

    lane = lax.broadcasted_iota(I32, (bm, LANES), 1)
    lane_f = lane.astype(F32)
    scores = jax.nn.sigmoid(lg_ref[...])
    sel = jnp.where(lane < n_experts, scores + bias_ref[...], -jnp.inf)

    hits, gates = [], []
    gsum = jnp.zeros((bm, 1), F32)
    member = jnp.zeros((bm, LANES), F32)
    idx_out = jnp.zeros((bm, LANES), F32)
    for k in range(TOP_K):
        m = jnp.max(sel, axis=1, keepdims=True)
        ik = jnp.min(jnp.where(sel == m, lane_f, float(LANES)), axis=1, keepdims=True)
        hit = lane_f == ik
        gk = jnp.sum(jnp.where(hit, scores, 0.0), axis=1, keepdims=True)
        sel = jnp.where(hit, -jnp.inf, sel)
        member = jnp.where(hit, 1.0, member)
        idx_out = jnp.where(lane == k, ik, idx_out)
        gsum = gsum + gk
        hits.append(hit)
        gates.append(gk)

    row = lax.broadcasted_iota(I32, (bm, bm), 0)
    col = lax.broadcasted_iota(I32, (bm, bm), 1)
    before = jnp.where(col < row, 1.0, 0.0).astype(BF16)
    prefix = jnp.dot(before, member.astype(BF16), preferred_element_type=F32) + carry_ref[...]

    gate_out = jnp.zeros((bm, LANES), F32)
    rank_out = jnp.zeros((bm, LANES), F32)
    for k in range(TOP_K):
        rk = jnp.sum(jnp.where(hits[k], prefix, 0.0), axis=1, keepdims=True)
        rank_out = jnp.where(lane == k, rk, rank_out)
        gate_out = jnp.where(lane == k, gates[k] / gsum * ROUTE_SCALE, gate_out)

    total = carry_ref[...] + jnp.sum(member, axis=0, keepdims=True)
    carry_ref[...] = total
    idx_ref[...] = idx_out.astype(I32)
    gate_ref[...] = gate_out
    rank_ref[...] = rank_out.astype(I32)
    cnt_ref[...] = jnp.broadcast_to(total, cnt_ref.shape).astype(I32)


def _route(logits, bias_pad, *, n_experts, bm=TOK_TILE):
    n = logits.shape[0]
    row = lambda i: (i, 0)
    fix = lambda i: (0, 0)
    return pl.pallas_call(
        functools.partial(_route_kernel, bm=bm, n_experts=n_experts),
        grid=(n // bm,),
        in_specs=[pl.BlockSpec((bm, LANES), row), pl.BlockSpec((1, LANES), fix)],
        out_specs=[pl.BlockSpec((bm, LANES), row), pl.BlockSpec((bm, LANES), row),
                   pl.BlockSpec((bm, LANES), row), pl.BlockSpec((8, LANES), fix)],
        out_shape=[jax.ShapeDtypeStruct((n, LANES), I32), jax.ShapeDtypeStruct((n, LANES), F32),
                   jax.ShapeDtypeStruct((n, LANES), I32), jax.ShapeDtypeStruct((8, LANES), I32)],
        scratch_shapes=[pltpu.VMEM((1, LANES), F32)],
        compiler_params=_cparams(("arbitrary",)),
        name="route_topk",
    )(logits, bias_pad)


def _dispatch_kernel(zs_ref, dest_ref, x_ref, xs_hbm, zero_ref, sem, zsem, *, bt, n_experts):
    i = pl.program_id(0)

    def zero_copy(e):
        zs = pl.multiple_of(zs_ref[e], SUB_ROWS)
        return pltpu.make_async_copy(zero_ref, xs_hbm.at[pl.ds(zs, SUB_ROWS), :], zsem)

    @pl.when(i == 0)
    def _():
        zero_ref[...] = jnp.zeros_like(zero_ref)

        def zstart(e, c):
            zero_copy(e).start()
            return c

        def zwait(e, c):
            zero_copy(e).wait()
            return c

        lax.fori_loop(0, n_experts, zstart, 0)
        lax.fori_loop(0, n_experts, zwait, 0)

    def start(g, c):
        for u in range(ROW_UNROLL):
            for k in range(TOP_K):
                dst = dest_ref[0, 0, (g * ROW_UNROLL + u) * TOP_K + k]
                pltpu.make_async_copy(x_ref.at[g, pl.ds(u, 1), :], xs_hbm.at[pl.ds(dst, 1), :],
                                      sem).start(priority=k % 2)
        return c

    lax.fori_loop(0, bt // ROW_UNROLL, start, 0)
    for k in range(TOP_K):
        pltpu.make_async_copy(x_ref, x_ref, sem).wait()


def _dispatch(zstart, dest3, x1p, *, rows_total, n_experts, bt=TOK_TILE):
    n, d = x1p.shape
    grid_spec = pltpu.PrefetchScalarGridSpec(
        num_scalar_prefetch=1,
        grid=(n // bt,),
        in_specs=[pl.BlockSpec((1, 1, bt * TOP_K), lambda i, zs: (i, 0, 0),
                               memory_space=pltpu.SMEM),
                  pl.BlockSpec((bt // ROW_UNROLL, ROW_UNROLL, d), lambda i, zs: (i, 0, 0))],
        out_specs=pl.BlockSpec(memory_space=pl.ANY),
        scratch_shapes=[pltpu.VMEM((SUB_ROWS, d), I32),
                        pltpu.SemaphoreType.DMA(()),
                        pltpu.SemaphoreType.DMA(())],
    )
    return pl.pallas_call(
        functools.partial(_dispatch_kernel, bt=bt, n_experts=n_experts),
        grid_spec=grid_spec,
        out_shape=jax.ShapeDtypeStruct((rows_total, d), I32),
        compiler_params=_cparams(("arbitrary",)),
        name="dispatch_rows",
    )(zstart, dest3, x1p.reshape(n // ROW_UNROLL, ROW_UNROLL, d))


def _experts_kernel(ce_ref, cv_ref, tot_ref, x_ref, w1_ref, w3_ref, w2_ref, o_ref,
                    w13_ref, acc_ref, h_ref, w2b_ref, *, n_up, fdim):
    c = pl.program_id(0)
    j = pl.program_id(1)
    nv = cv_ref[c]
    del ce_ref, tot_ref
    all_rows = slice(0, ROW_CHUNK)
    sub_blocks = [slice(sb * SUB_ROWS, (sb + 1) * SUB_ROWS) for sb in range(ROW_CHUNK // SUB_ROWS)]

    def up_rows(rows, first, last):
        xb = _unpack_bf16_pairs(x_ref[rows, :]).astype(BF16)
        part = jnp.dot(xb, w13_ref[...], preferred_element_type=F32)
        if not first:
            part = part + acc_ref[rows, :]
        if last:
            h1 = part[:, :fdim]
            h_ref[rows, :] = (h1 * jax.nn.sigmoid(h1) * part[:, fdim:]).astype(BF16)
        else:
            acc_ref[rows, :] = part

    def cast_up_weights():
        w13_ref[:, :fdim] = w1_ref[...].astype(BF16)
        w13_ref[:, fdim:] = w3_ref[...].astype(BF16)

    def up(first, last):
        @pl.when(nv == ROW_CHUNK)
        def _():
            cast_up_weights()
            up_rows(all_rows, first, last)

        @pl.when(nv < ROW_CHUNK)
        def _():
            cast_up_weights()
            for rows in sub_blocks:
                @pl.when(rows.start < nv)
                def _():
                    up_rows(rows, first, last)

    @pl.when(j == 0)
    def _():
        up(True, n_up == 1)

    if n_up > 2:
        @pl.when((j > 0) & (j < n_up - 1))
        def _():
            up(False, False)

    if n_up > 1:
        @pl.when(j == n_up - 1)
        def _():
            up(False, True)

    @pl.when(j >= n_up)
    def _():
        @pl.when(nv == ROW_CHUNK)
        def _():
            w2b_ref[...] = w2_ref[...].astype(BF16)
            o_ref[...] = _pack_bf16_pairs(
                jnp.dot(h_ref[...], w2b_ref[...], preferred_element_type=F32))

        @pl.when(nv < ROW_CHUNK)
        def _():
            w2b_ref[...] = w2_ref[...].astype(BF16)
            for rows in sub_blocks:
                @pl.when(rows.start < nv)
                def _():
                    o_ref[rows, :] = _pack_bf16_pairs(
                        jnp.dot(h_ref[rows, :], w2b_ref[...], preferred_element_type=F32))

                @pl.when(rows.start >= nv)
                def _():
                    o_ref[rows, :] = jnp.zeros((SUB_ROWS, o_ref.shape[1]), I32)


def _experts(chunk_expert, chunk_valid, total, xs, w1, w3, w2, *, n_chunks):
    d = w1.shape[1]
    fdim = w1.shape[2]
    n_up = d // UP_K_TILE
    n_down = d // DOWN_N_TILE

    def x_map(c, j, ce, cv, tot):
        return (jnp.minimum(c, tot[0] - 1), jnp.minimum(j, n_up - 1))

    def w13_map(c, j, ce, cv, tot):
        return (ce[c], jnp.minimum(j, n_up - 1), 0)

    def w2_map(c, j, ce, cv, tot):
        return (ce[c], 0, jnp.maximum(j - n_up, 0))

    def o_map(c, j, ce, cv, tot):
        return (c, jnp.maximum(j - n_up, 0))

    grid_spec = pltpu.PrefetchScalarGridSpec(
        num_scalar_prefetch=3,
        grid=(n_chunks, n_up + n_down),
        in_specs=[pl.BlockSpec((ROW_CHUNK, UP_K_TILE // 2), x_map),
                  pl.BlockSpec((None, UP_K_TILE, fdim), w13_map),
                  pl.BlockSpec((None, UP_K_TILE, fdim), w13_map),
                  pl.BlockSpec((None, fdim, DOWN_N_TILE), w2_map)],
        out_specs=pl.BlockSpec((ROW_CHUNK, DOWN_N_TILE // 2), o_map),
        scratch_shapes=[pltpu.VMEM((UP_K_TILE, 2 * fdim), BF16),
                        pltpu.VMEM((ROW_CHUNK, 2 * fdim), F32),
                        pltpu.VMEM((ROW_CHUNK, fdim), BF16),
                        pltpu.VMEM((fdim, DOWN_N_TILE), BF16)],
    )
    return pl.pallas_call(
        functools.partial(_experts_kernel, n_up=n_up, fdim=fdim),
        grid_spec=grid_spec,
        out_shape=jax.ShapeDtypeStruct((n_chunks * ROW_CHUNK, d // 2), I32),
        compiler_params=_cparams(("arbitrary", "arbitrary")),
        name="routed_experts",
    )(chunk_expert, chunk_valid, total, xs, w1, w3, w2)


def _shared_kernel(x_ref, w13_ref, w2_ref, o_ref, *, tf):
    f = pl.program_id(1)

    @pl.when(f == 0)
    def _():
        o_ref[...] = jnp.zeros_like(o_ref)

    rows_per = x_ref.shape[0] // ROW_GROUPS
    for g in range(ROW_GROUPS):
        rows = slice(g * rows_per, (g + 1) * rows_per)
        r = jnp.dot(x_ref[rows, :], w13_ref[...], preferred_element_type=F32)
        h1 = r[:, :tf]
        h = (h1 * jax.nn.sigmoid(h1) * r[:, tf:]).astype(BF16)
        o_ref[rows, :] += jnp.dot(h, w2_ref[...], preferred_element_type=F32)


def _shared_expert(x1b, w13s, w2s, *, bm=1024, tf=SHARED_F_TILE):
    n, d = x1b.shape
    nf = w2s.shape[0] // tf
    return pl.pallas_call(
        functools.partial(_shared_kernel, tf=tf),
        grid=(n // bm, nf),
        in_specs=[pl.BlockSpec((bm, d), lambda i, f: (i, 0)),
                  pl.BlockSpec((d, 2 * tf), lambda i, f: (0, f)),
                  pl.BlockSpec((tf, d), lambda i, f: (f, 0))],
        out_specs=pl.BlockSpec((bm, d), lambda i, f: (i, 0)),
        out_shape=jax.ShapeDtypeStruct((n, d), F32),
        compiler_params=_cparams(("parallel", "arbitrary")),
        name="shared_expert",
    )(x1b, w13s, w2s)


def _combine_kernel(dcur_ref, dnext_ref, x1_ref, ysh_ref, gate_ref, g_ref, b_ref, ys_hbm, o_ref,
                    buf_ref, sem, *, bt):
    i = pl.program_id(0)
    slot = lax.rem(i, 2)

    def issue(dref, s):
        def start(g, c):
            for u in range(ROW_UNROLL):
                for k in range(TOP_K):
                    src = dref[0, 0, (g * ROW_UNROLL + u) * TOP_K + k]
                    pltpu.make_async_copy(ys_hbm.at[pl.ds(src, 1), :],
                                          buf_ref.at[s, k, g, pl.ds(u, 1), :],
                                          sem.at[s]).start(priority=k % 2)
            return c

        lax.fori_loop(0, bt // ROW_UNROLL, start, 0)

    @pl.when(i == 0)
    def _():
        issue(dcur_ref, slot)

    @pl.when(i + 1 < pl.num_programs(0))
    def _():
        issue(dnext_ref, 1 - slot)

    acc = ALPHA * x1_ref[...] + ysh_ref[...]

    for k in range(TOP_K):
        pltpu.make_async_copy(buf_ref.at[slot, k], buf_ref.at[slot, k], sem.at[slot]).wait()
    gate = gate_ref[...]
    half = PACK_TILE // 2
    for k in range(TOP_K):
        yk = jnp.concatenate(
            [_unpack_bf16_pairs(buf_ref[slot, k, :, :, t * half:(t + 1) * half].reshape(bt, half))
             for t in range(acc.shape[1] // PACK_TILE)], axis=1)
        acc = acc + gate[:, k:k + 1] * yk
    o_ref[...] = _layer_norm(acc, g_ref[...], b_ref[...])


def _combine(dest3, x1, ysh, gate, g2, b2, ys, *, bt=TOK_TILE):
    n, d = x1.shape
    nt = n // bt
    row = lambda i: (i, 0)
    fix = lambda i: (0, 0)
    return pl.pallas_call(
        functools.partial(_combine_kernel, bt=bt),
        grid=(nt,),
        in_specs=[pl.BlockSpec((1, 1, bt * TOP_K), lambda i: (i, 0, 0), memory_space=pltpu.SMEM),
                  pl.BlockSpec((1, 1, bt * TOP_K), lambda i: (jnp.minimum(i + 1, nt - 1), 0, 0),
                               memory_space=pltpu.SMEM),
                  pl.BlockSpec((bt, d), row), pl.BlockSpec((bt, d), row),
                  pl.BlockSpec((bt, LANES), row),
                  pl.BlockSpec((1, d), fix), pl.BlockSpec((1, d), fix),
                  pl.BlockSpec(memory_space=pl.ANY)],
        out_specs=pl.BlockSpec((bt, d), row),
        out_shape=jax.ShapeDtypeStruct((n, d), F32),
        scratch_shapes=[pltpu.VMEM((2, TOP_K, bt // ROW_UNROLL, ROW_UNROLL, d // 2), I32),
                        pltpu.SemaphoreType.DMA((2,))],
        compiler_params=_cparams(("arbitrary",)),
        name="combine_ln2",
    )(dest3, dest3, x1, ysh, gate, g2, b2, ys)


def _layer(x, w_in, b_forget, pool_w, pool_scale, w_branch_pool, w_branch_attn, w_out,
           ln1_g, ln1_b, w_router, router_bias, w1, w3, w2, w_shared1, w_shared3, w_shared2,
           ln2_g, ln2_b):
    batch, seq, d = x.shape
    n = batch * seq
    pool_width = w_branch_pool.shape[0]
    attn_width = w_branch_attn.shape[0]
    n_heads = attn_width // HEAD_DIM
    n_experts = w1.shape[0]
    off_q = pool_width
    off_k = off_q + attn_width
    off_v = off_k + attn_width
    off_f = off_v + attn_width
    off_gate = off_f + n_heads

    x2 = x.reshape(n, d)

    w_main = w_in[:, :off_f].astype(BF16)
    wf_pad = jnp.pad(w_in[:, off_f:off_gate], ((0, 0), (0, LANES - n_heads))).astype(BF16)
    bf_pad = jnp.pad(b_forget.astype(F32), (0, LANES - n_heads)).reshape(1, LANES)
    w_gate = w_in[:, off_gate:].astype(BF16)
    piece = jnp.arange(3)[:, None, None]
    src = jnp.arange(LANES)[None, :, None]
    dst = jnp.arange(n_heads * LANES)[None, None, :]
    sel = ((src < n_heads) & (dst == src * LANES + piece)).astype(BF16)
    wr = jnp.pad(w_router.astype(F32), ((0, 0), (0, LANES - n_experts)))
    wr_hi = wr.astype(BF16)
    wr_lo = (wr - wr_hi.astype(F32)).astype(BF16)
    rb_pad = jnp.pad(router_bias.astype(F32), (0, LANES - n_experts)).reshape(1, LANES)
    tf = SHARED_F_TILE
    fs = w_shared1.shape[1]
    w13s = jnp.concatenate([w_shared1.reshape(d, fs // tf, tf), w_shared3.reshape(d, fs // tf, tf)],
                           axis=2).reshape(d, 2 * fs).astype(BF16)

    z = _inproj(x2, w_main, q_lo=off_q, q_hi=off_k, q_scale=HEAD_DIM ** -0.5)
    ka, edge = _forget_bias(x2, wf_pad, bf_pad, sel, batch=batch, seq=seq, n_heads=n_heads)
    n_kblk = seq // ATTN_BLOCK
    edge = edge.reshape(batch, n_kblk, 8, LANES)[:, :, :2, :n_heads]
    bfirst = edge[:, :, 0, :].transpose(0, 2, 1).reshape(-1)
    blast = edge[:, :, 1, :].transpose(0, 2, 1).reshape(-1)
    pool_o = _pool(z, pool_w.astype(BF16), pool_scale.astype(F32).reshape(1, pool_width),
                   batch=batch, seq=seq, width=pool_width)
    attn_o = _attention(bfirst, blast, z, ka, batch=batch, seq=seq, n_heads=n_heads,
                        q_col=off_q // HEAD_DIM, k_col=off_k // HEAD_DIM, v_col=off_v // HEAD_DIM)
    merged = _merge(x2, pool_o, attn_o, w_gate, w_branch_pool.astype(BF16),
                    w_branch_attn.astype(BF16))
    x1, x1b, x1p, logits = _out_ln_router(x2, merged, w_out.astype(BF16),
                                     ln1_g.astype(F32).reshape(1, d), ln1_b.astype(F32).reshape(1, d),
                                     wr_hi, wr_lo)

    idx, gate, rank, cnt = _route(logits, rb_pad, n_experts=n_experts)
    counts = cnt[0, :n_experts]
    n_assign = n * TOP_K
    n_chunks = -(-n_assign // ROW_CHUNK) + n_experts
    chunks_per = (counts + ROW_CHUNK - 1) // ROW_CHUNK
    chunk_end = jnp.cumsum(chunks_per)
    chunk_begin = chunk_end - chunks_per
    total = chunk_end[-1]
    group_start = chunk_begin * ROW_CHUNK
    cids = jnp.arange(n_chunks, dtype=I32)
    c_eff = jnp.minimum(cids, total - 1)
    chunk_expert = jnp.minimum(jnp.sum((chunk_end[None, :] <= c_eff[:, None]).astype(I32), axis=1),
                               n_experts - 1).astype(I32)
    chunk_valid = jnp.where(cids < total,
                            jnp.minimum(counts[chunk_expert] - (cids - chunk_begin[chunk_expert]) * ROW_CHUNK,
                                        ROW_CHUNK), 0).astype(I32)
    idx_k = idx[:, :TOP_K]
    dest = group_start[idx_k].astype(I32) + rank[:, :TOP_K]
    dest3 = dest.reshape(n // TOK_TILE, 1, TOK_TILE * TOP_K)
    zstart = ((group_start + counts) // SUB_ROWS * SUB_ROWS).astype(I32)
    rows_total = (n_chunks + 1) * ROW_CHUNK

    xs = _dispatch(zstart, dest3, x1p, rows_total=rows_total, n_experts=n_experts)
    ys = _experts(chunk_expert, chunk_valid, total.reshape(1).astype(I32), xs, w1, w3, w2,
                  n_chunks=n_chunks)
    ysh = _shared_expert(x1b, w13s, w_shared2.astype(BF16))
    out = _combine(dest3, x1, ysh, gate, ln2_g.astype(F32).reshape(1, d),
                   ln2_b.astype(F32).reshape(1, d), ys)
    return out.reshape(batch, seq, d)


def kernel(x, w_in, b_forget, pool_w, pool_scale, w_branch_pool, w_branch_attn, w_out, ln1_g, ln1_b,
           w_router, router_bias, w1, w3, w2, w_shared1, w_shared3, w_shared2, ln2_g, ln2_b):
    for l in range(w_in.shape[0]):
        x = _layer(x, w_in[l], b_forget[l], pool_w[l], pool_scale[l], w_branch_pool[l],
                   w_branch_attn[l], w_out[l], ln1_g[l], ln1_b[l], w_router[l], router_bias[l],
                   w1[l], w3[l], w2[l], w_shared1[l], w_shared3[l], w_shared2[l], ln2_g[l], ln2_b[l])
    return x
```

```python
import functools

import jax
import jax.numpy as jnp
from jax import lax
from jax.experimental import pallas as pl
from jax.experimental.pallas import tpu as pltpu

F32 = jnp.float32
BF16 = jnp.bfloat16
I32 = jnp.int32

POOL_WINDOWS = (2, 4, 8, 16)
HEAD_DIM = 128
TOP_K = 6
ROUTE_SCALE = 1.0
DEPTH = 1
ALPHA = (2 * DEPTH) ** 0.25
LN_EPS = 1e-5

LANES = 128
MAX_WINDOW = max(POOL_WINDOWS)
ATTN_BLOCK = 512
Q_HALVES = 2
SKIP_MARGIN = 110.0
ROW_CHUNK = 1024
SUB_ROWS = 256
PACK_TILE = 512
UP_K_TILE = PACK_TILE
DOWN_N_TILE = 2 * PACK_TILE
SHARED_F_TILE = 256
TOK_TILE = 256
ROW_UNROLL = 8
ROW_GROUPS = 2
VMEM_LIMIT = 56 * 1024 * 1024


def _cparams(sem, vmem=VMEM_LIMIT):
    return pltpu.CompilerParams(dimension_semantics=sem, vmem_limit_bytes=vmem)


def _split3(v):
    hi = v.astype(BF16)
    r1 = v - hi.astype(F32)
    mid = r1.astype(BF16)
    lo = (r1 - mid.astype(F32)).astype(BF16)
    return hi, mid, lo


def _pack_bf16_pairs(v):
    c = v.shape[1] // 2
    lo = lax.bitcast_convert_type(v[:, :c].astype(BF16).astype(F32), I32)
    hi = lax.bitcast_convert_type(v[:, c:].astype(BF16).astype(F32), I32)
    return ((lo >> 16) & 0xFFFF) | (hi & -65536)


def _unpack_bf16_pairs(w):
    lo = lax.bitcast_convert_type(w << 16, F32)
    hi = lax.bitcast_convert_type(w & -65536, F32)
    return jnp.concatenate([lo, hi], axis=1)


def _pack_tiles(v):
    return jnp.concatenate([_pack_bf16_pairs(v[:, g * PACK_TILE:(g + 1) * PACK_TILE])
                            for g in range(v.shape[1] // PACK_TILE)], axis=1)


def _layer_norm(v, g, b):
    mu = jnp.mean(v, axis=-1, keepdims=True)
    d = v - mu
    var = jnp.mean(d * d, axis=-1, keepdims=True)
    return d * lax.rsqrt(var + LN_EPS) * g + b


def _inproj_kernel(x_ref, w_ref, o_ref, xb_ref, *, bn, q_lo, q_hi, q_scale):
    j = pl.program_id(1)

    @pl.when(j == 0)
    def _():
        xb_ref[...] = x_ref[...].astype(BF16)

    acc = jnp.dot(xb_ref[...], w_ref[...], preferred_element_type=F32)
    col0 = j * bn
    s = jnp.where((col0 >= q_lo) & (col0 < q_hi), q_scale, 1.0).astype(F32)
    o_ref[...] = (acc * s).astype(BF16)


def _inproj(x2, w_main, *, q_lo, q_hi, q_scale, bm=1024, bn=512):
    n, d = x2.shape
    c = w_main.shape[1]
    return pl.pallas_call(
        functools.partial(_inproj_kernel, bn=bn, q_lo=q_lo, q_hi=q_hi, q_scale=q_scale),
        grid=(n // bm, c // bn),
        in_specs=[pl.BlockSpec((bm, d), lambda i, j: (i, 0)),
                  pl.BlockSpec((d, bn), lambda i, j: (0, j))],
        out_specs=pl.BlockSpec((bm, bn), lambda i, j: (i, j)),
        out_shape=jax.ShapeDtypeStruct((n, c), BF16),
        scratch_shapes=[pltpu.VMEM((bm, d), BF16)],
        compiler_params=_cparams(("parallel", "arbitrary")),
        name="inproj",
    )(x2, w_main)


def _forget_kernel(x_ref, wf_ref, b_ref, sel_ref, ka_ref, edge_ref, carry_ref, *, ts, n_heads):
    i = pl.program_id(1)

    @pl.when(i == 0)
    def _():
        carry_ref[...] = jnp.zeros_like(carry_ref)

    z = jnp.dot(x_ref[...].astype(BF16), wf_ref[...], preferred_element_type=F32) + b_ref[...]
    lf = jnp.minimum(z, 0.0) - jnp.log1p(jnp.exp(-jnp.abs(z)))
    lane = lax.broadcasted_iota(I32, lf.shape, 1)
    lf = jnp.where(lane < n_heads, lf, 0.0)

    row = lax.broadcasted_iota(I32, (ts, ts), 0)
    col = lax.broadcasted_iota(I32, (ts, ts), 1)
    tri = jnp.where(col <= row, 1.0, 0.0).astype(BF16)
    hi, mid, lo = _split3(lf)
    cs = (jnp.dot(tri, hi, preferred_element_type=F32)
          + jnp.dot(tri, mid, preferred_element_type=F32)
          + jnp.dot(tri, lo, preferred_element_type=F32)) + carry_ref[...]
    carry_ref[...] = cs[ts - 1:ts, :]
    edge_ref[0] = jnp.broadcast_to(-cs[ts - 1:ts, :], (8, LANES))

    h2, m2, l2 = _split3(-cs)
    ka = (jnp.dot(h2, sel_ref[0], preferred_element_type=F32)
          + jnp.dot(m2, sel_ref[1], preferred_element_type=F32)
          + jnp.dot(l2, sel_ref[2], preferred_element_type=F32))
    ka_ref[...] = ka.astype(BF16)


def _forget_bias(x2, wf_pad, b_pad, sel, *, batch, seq, n_heads, ts=ATTN_BLOCK):
    n, d = x2.shape
    nt = seq // ts
    return pl.pallas_call(
        functools.partial(_forget_kernel, ts=ts, n_heads=n_heads),
        grid=(batch, nt),
        in_specs=[pl.BlockSpec((ts, d), lambda b, i: (b * nt + i, 0)),
                  pl.BlockSpec((d, LANES), lambda b, i: (0, 0)),
                  pl.BlockSpec((1, LANES), lambda b, i: (0, 0)),
                  pl.BlockSpec((3, LANES, n_heads * LANES), lambda b, i: (0, 0, 0))],
        out_specs=[pl.BlockSpec((ts, n_heads * LANES), lambda b, i: (b * nt + i, 0)),
                   pl.BlockSpec((1, 8, LANES), lambda b, i: (b * nt + i, 0, 0))],
        out_shape=[jax.ShapeDtypeStruct((n, n_heads * LANES), BF16),
                   jax.ShapeDtypeStruct((batch * nt, 8, LANES), F32)],
        scratch_shapes=[pltpu.VMEM((1, LANES), F32)],
        compiler_params=_cparams(("arbitrary", "arbitrary")),
        name="forget_bias",
    )(x2, wf_pad, b_pad, sel)


def _pool_kernel(u_ref, halo_ref, pw_ref, sc_ref, o_ref, buf_ref, *, ts, gdim):
    i = pl.program_id(1)
    h = MAX_WINDOW
    halo = halo_ref[...].astype(F32)
    buf_ref[0:h, :] = jnp.where(i > 0, halo, 0.0)
    buf_ref[h:h + ts, :] = u_ref[...].astype(F32)

    t = i * ts + lax.broadcasted_iota(I32, (ts, 1), 0)
    for g, w in enumerate(POOL_WINDOWS):
        cs = slice(g * gdim, (g + 1) * gdim)
        tok = buf_ref[h:h + ts, cs]
        wsum = tok
        for k in range(1, w):
            wsum = wsum + buf_ref[h - k:h - k + ts, cs]
        cnt = jnp.minimum(t + 1, w).astype(F32)
        p = (wsum / cnt - tok).astype(BF16)
        y = jnp.dot(p, pw_ref[g], preferred_element_type=F32) * sc_ref[:, cs]
        o_ref[:, cs] = y.astype(BF16)


def _pool(z, pool_w_b, pool_scale2, *, batch, seq, width, ts=512):
    n = z.shape[0]
    groups = len(POOL_WINDOWS)
    gdim = width // groups
    nt = seq // ts
    hb = ts // MAX_WINDOW
    return pl.pallas_call(
        functools.partial(_pool_kernel, ts=ts, gdim=gdim),
        grid=(batch, nt),
        in_specs=[pl.BlockSpec((ts, width), lambda b, i: (b * nt + i, 0)),
                  pl.BlockSpec((MAX_WINDOW, width),
                               lambda b, i: (jnp.maximum((b * nt + i) * hb - 1, 0), 0)),
                  pl.BlockSpec((groups, gdim, gdim), lambda b, i: (0, 0, 0)),
                  pl.BlockSpec((1, width), lambda b, i: (0, 0))],
        out_specs=pl.BlockSpec((ts, width), lambda b, i: (b * nt + i, 0)),
        out_shape=jax.ShapeDtypeStruct((n, width), BF16),
        scratch_shapes=[pltpu.VMEM((ts + MAX_WINDOW, width), F32)],
        compiler_params=_cparams(("parallel", "parallel")),
        name="pool_mixer",
    )(z, z, pool_w_b, pool_scale2)


def _attn_kernel(blast_ref, q_ref, k_ref, ka_ref, v_ref, o_ref,
                 qa_ref, m_ref, l_ref, acc_ref, kmax_ref, *, blk, n_kblk):
    qi = pl.program_id(2)
    base = (pl.program_id(0) * pl.num_programs(1) + pl.program_id(1)) * n_kblk

    @pl.when(qi == 0)
    def _():
        def knorm(c, run):
            kk = k_ref[pl.ds(pl.multiple_of(c * blk, blk), blk), :].astype(F32)
            return jnp.maximum(run, jnp.sum(kk * kk, axis=1, keepdims=True))

        run = lax.fori_loop(0, n_kblk, knorm, jnp.zeros((blk, 1), F32))
        kmax_ref[0] = jnp.max(jnp.sqrt(run))

    lane = lax.broadcasted_iota(I32, (blk, HEAD_DIM), 1)
    ones3 = jnp.where(lane < 3, 1.0, 0.0).astype(BF16)
    for h in range(Q_HALVES):
        qa_ref[h] = jnp.concatenate([q_ref[h * blk:(h + 1) * blk, :], ones3], axis=1)
    m_ref[...] = jnp.full_like(m_ref, -jnp.inf)
    l_ref[...] = jnp.zeros_like(l_ref)
    acc_ref[...] = jnp.zeros_like(acc_ref)
    reps = blk // LANES

    def step(h, ki, masked):
        ks = pl.multiple_of(ki * blk, blk)
        k_aug = jnp.concatenate([k_ref[pl.ds(ks, blk), :], ka_ref[pl.ds(ks, blk), :]], axis=1)
        s = lax.dot_general(qa_ref[h], k_aug, (((1,), (1,)), ((), ())),
                            preferred_element_type=F32)
        if masked:
            r = lax.broadcasted_iota(I32, (blk, blk), 0)
            c = lax.broadcasted_iota(I32, (blk, blk), 1)
            s = jnp.where(c <= r, s, -jnp.inf)
        m_prev = m_ref[h]
        m_next = jnp.maximum(m_prev, jnp.max(s, axis=1, keepdims=True))
        alpha = jnp.exp(m_prev - m_next)
        p = jnp.exp(s - jnp.concatenate([m_next] * reps, axis=1))
        l_ref[h] = alpha * l_ref[h] + jnp.sum(p, axis=1, keepdims=True)
        acc_ref[h] = alpha * acc_ref[h] + jnp.dot(
            p.astype(BF16), v_ref[pl.ds(ks, blk), :], preferred_element_type=F32)
        m_ref[h] = m_next

    def body(kq, carry):
        for d in range(Q_HALVES):
            for h in range(Q_HALVES):
                step(h, Q_HALVES * kq + d, False)
        return carry

    for h in range(Q_HALVES):
        for d in range(h):
            step(h, Q_HALVES * qi + d, False)
        step(h, Q_HALVES * qi + h, True)

    qq = q_ref[...].astype(F32)
    qmax = jnp.max(jnp.sqrt(jnp.sum(qq * qq, axis=1, keepdims=True)))
    thr = jnp.min(m_ref[...]) - SKIP_MARGIN - qmax * kmax_ref[0]

    def count_skippable(j, a):
        return a + jnp.where(blast_ref[base + j] < thr, 1, 0).astype(I32)

    n_skip = lax.fori_loop(0, Q_HALVES * qi, count_skippable, jnp.int32(0))
    lax.fori_loop(n_skip // Q_HALVES, qi, body, 0)
    for h in range(Q_HALVES):
        o_ref[h * blk:(h + 1) * blk, :] = (acc_ref[h] / l_ref[h]).astype(BF16)


def _attention(blast, z, ka, *, batch, seq, n_heads, q_col, k_col, v_col, blk=ATTN_BLOCK):
    n = z.shape[0]
    qb = Q_HALVES * blk
    nq = seq // qb
    grid_spec = pltpu.PrefetchScalarGridSpec(
        num_scalar_prefetch=1,
        grid=(batch, n_heads, nq),
        in_specs=[pl.BlockSpec((qb, HEAD_DIM), lambda b, h, i, bl: (b * nq + i, q_col + h)),
                  pl.BlockSpec((seq, HEAD_DIM), lambda b, h, i, bl: (b, k_col + h)),
                  pl.BlockSpec((seq, HEAD_DIM), lambda b, h, i, bl: (b, h)),
                  pl.BlockSpec((seq, HEAD_DIM), lambda b, h, i, bl: (b, v_col + h))],
        out_specs=pl.BlockSpec((qb, HEAD_DIM), lambda b, h, i, bl: (b * nq + i, h)),
        scratch_shapes=[pltpu.VMEM((Q_HALVES, blk, 2 * HEAD_DIM), BF16),
                        pltpu.VMEM((Q_HALVES, blk, LANES), F32),
                        pltpu.VMEM((Q_HALVES, blk, LANES), F32),
                        pltpu.VMEM((Q_HALVES, blk, HEAD_DIM), F32),
                        pltpu.SMEM((1,), F32)],
    )
    return pl.pallas_call(
        functools.partial(_attn_kernel, blk=blk, n_kblk=seq // blk),
        grid_spec=grid_spec,
        out_shape=jax.ShapeDtypeStruct((n, n_heads * HEAD_DIM), BF16),
        compiler_params=_cparams(("arbitrary", "arbitrary", "arbitrary")),
        name="fox_attention",
    )(blast, z, z, ka, z)


def _merge_kernel(x_ref, p_ref, a_ref, wg0_ref, wg1_ref, wbp_ref, wba_ref, o_ref, xb_ref):
    j = pl.program_id(1)

    @pl.when(j == 0)
    def _():
        xb_ref[...] = x_ref[...].astype(BF16)

    xb = xb_ref[...]
    g0 = jax.nn.sigmoid(jnp.dot(xb, wg0_ref[...], preferred_element_type=F32))
    g1 = jax.nn.sigmoid(jnp.dot(xb, wg1_ref[...], preferred_element_type=F32))
    yp = jnp.dot(p_ref[...], wbp_ref[...], preferred_element_type=F32)
    ya = jnp.dot(a_ref[...], wba_ref[...], preferred_element_type=F32)
    o_ref[...] = (g0 * yp + g1 * ya).astype(BF16)


def _merge(x2, pool_o, attn_o, wg, wbp, wba, *, bm=1024, bn=512):
    n, d = x2.shape
    wp = pool_o.shape[1]
    wa = attn_o.shape[1]
    nj = d // bn
    return pl.pallas_call(
        _merge_kernel,
        grid=(n // bm, nj),
        in_specs=[pl.BlockSpec((bm, d), lambda i, j: (i, 0)),
                  pl.BlockSpec((bm, wp), lambda i, j: (i, 0)),
                  pl.BlockSpec((bm, wa), lambda i, j: (i, 0)),
                  pl.BlockSpec((d, bn), lambda i, j: (0, j)),
                  pl.BlockSpec((d, bn), lambda i, j: (0, nj + j)),
                  pl.BlockSpec((wp, bn), lambda i, j: (0, j)),
                  pl.BlockSpec((wa, bn), lambda i, j: (0, j))],
        out_specs=pl.BlockSpec((bm, bn), lambda i, j: (i, j)),
        out_shape=jax.ShapeDtypeStruct((n, d), BF16),
        scratch_shapes=[pltpu.VMEM((bm, d), BF16)],
        compiler_params=_cparams(("parallel", "arbitrary")),
        name="branch_merge",
    )(x2, pool_o, attn_o, wg, wg, wbp, wba)


def _out_ln_router_kernel(x_ref, m_ref, wo_ref, g_ref, b_ref, wrh_ref, wrl_ref,
                          x1_ref, x1b_ref, x1p_ref, lg_ref):
    bm = x_ref.shape[0]
    rows_per = bm // ROW_GROUPS
    half = PACK_TILE // 2
    for r in range(ROW_GROUPS):
        rows = slice(r * rows_per, (r + 1) * rows_per)
        mix = jnp.dot(m_ref[rows, :], wo_ref[...], preferred_element_type=F32)
        x1 = _layer_norm(ALPHA * x_ref[rows, :] + mix, g_ref[...], b_ref[...])
        x1_ref[rows, :] = x1
        hi = x1.astype(BF16)
        x1b_ref[rows, :] = hi
        for t in range(x1.shape[1] // PACK_TILE):
            x1p_ref[rows, t * half:(t + 1) * half] = _pack_bf16_pairs(
                x1[:, t * PACK_TILE:(t + 1) * PACK_TILE])
        lo = (x1 - hi.astype(F32)).astype(BF16)
        lg_ref[rows, :] = (jnp.dot(hi, wrh_ref[...], preferred_element_type=F32)
                           + jnp.dot(lo, wrh_ref[...], preferred_element_type=F32)
                           + jnp.dot(hi, wrl_ref[...], preferred_element_type=F32))


def _out_ln_router(x2, merged, wo, g1, b1, wr_hi, wr_lo, *, bm=512):
    n, d = x2.shape
    row = lambda i: (i, 0)
    fix = lambda i: (0, 0)
    return pl.pallas_call(
        _out_ln_router_kernel,
        grid=(n // bm,),
        in_specs=[pl.BlockSpec((bm, d), row), pl.BlockSpec((bm, d), row),
                  pl.BlockSpec((d, d), fix), pl.BlockSpec((1, d), fix), pl.BlockSpec((1, d), fix),
                  pl.BlockSpec((d, LANES), fix), pl.BlockSpec((d, LANES), fix)],
        out_specs=[pl.BlockSpec((bm, d), row), pl.BlockSpec((bm, d), row),
                   pl.BlockSpec((bm, d // 2), row), pl.BlockSpec((bm, LANES), row)],
        out_shape=[jax.ShapeDtypeStruct((n, d), F32), jax.ShapeDtypeStruct((n, d), BF16),
                   jax.ShapeDtypeStruct((n, d // 2), I32), jax.ShapeDtypeStruct((n, LANES), F32)],
        compiler_params=_cparams(("parallel",)),
        name="outproj_ln1_router",
    )(x2, merged, wo, g1, b1, wr_hi, wr_lo)


def _route_kernel(lg_ref, bias_ref, idx_ref, gate_ref, rank_ref, cnt_ref, carry_ref,
                  *, bm, n_experts):
    i = pl.program_id(0)

    @pl.when(i == 0)
    def _():
        carry_ref[...] = jnp.zeros_like(carry_ref)

    lane = lax.broadcasted_iota(I32, (bm, LANES), 1)
    lane_f = lane.astype(F32)
    scores = jax.nn.sigmoid(lg_ref[...])
    sel = jnp.where(lane < n_experts, scores + bias_ref[...], -jnp.inf)

    hits, gates = [], []
    gsum = jnp.zeros((bm, 1), F32)
    member = jnp.zeros((bm, LANES), F32)
    idx_out = jnp.zeros((bm, LANES), F32)
    for k in range(TOP_K):
        m = jnp.max(sel, axis=1, keepdims=True)
        ik = jnp.min(jnp.where(sel == m, lane_f, float(LANES)), axis=1, keepdims=True)
        hit = lane_f == ik
        gk = jnp.sum(jnp.where(hit, scores, 0.0), axis=1, keepdims=True)
        sel = jnp.where(hit, -jnp.inf, sel)
        member = jnp.where(hit, 1.0, member)
        idx_out = jnp.where(lane == k, ik, idx_out)
        gsum = gsum + gk
        hits.append(hit)
        gates.append(gk)

    row = lax.broadcasted_iota(I32, (bm, bm), 0)
    col = lax.broadcasted_iota(I32, (bm, bm), 1)
    before = jnp.where(col < row, 1.0, 0.0).astype(BF16)
    prefix = jnp.dot(before, member.astype(BF16), preferred_element_type=F32) + carry_ref[...]

    gate_out = jnp.zeros((bm, LANES), F32)
    rank_out = jnp.zeros((bm, LANES), F32)
    for k in range(TOP_K):
        rk = jnp.sum(jnp.where(hits[k], prefix, 0.0), axis=1, keepdims=True)
        rank_out = jnp.where(lane == k, rk, rank_out)
        gate_out = jnp.where(lane == k, gates[k] / gsum * ROUTE_SCALE, gate_out)

    total = carry_ref[...] + jnp.sum(member, axis=0, keepdims=True)
    carry_ref[...] = total
    idx_ref[...] = idx_out.astype(I32)
    gate_ref[...] = gate_out
    rank_ref[...] = rank_out.astype(I32)
    cnt_ref[...] = jnp.broadcast_to(total, cnt_ref.shape).astype(I32)


def _route(logits, bias_pad, *, n_experts, bm=TOK_TILE):
    n = logits.shape[0]
    row = lambda i: (i, 0)
    fix = lambda i: (0, 0)
    return pl.pallas_call(
        functools.partial(_route_kernel, bm=bm, n_experts=n_experts),
        grid=(n // bm,),
        in_specs=[pl.BlockSpec((bm, LANES), row), pl.BlockSpec((1, LANES), fix)],
        out_specs=[pl.BlockSpec((bm, LANES), row), pl.BlockSpec((bm, LANES), row),
                   pl.BlockSpec((bm, LANES), row), pl.BlockSpec((8, LANES), fix)],
        out_shape=[jax.ShapeDtypeStruct((n, LANES), I32), jax.ShapeDtypeStruct((n, LANES), F32),
                   jax.ShapeDtypeStruct((n, LANES), I32), jax.ShapeDtypeStruct((8, LANES), I32)],
        scratch_shapes=[pltpu.VMEM((1, LANES), F32)],
        compiler_params=_cparams(("arbitrary",)),
        name="route_topk",
    )(logits, bias_pad)


def _dispatch_kernel(zs_ref, dest_ref, x_ref, xs_hbm, zero_ref, sem, zsem, *, bt, n_experts):
    i = pl.program_id(0)

    def zero_copy(e):
        zs = pl.multiple_of(zs_ref[e], SUB_ROWS)
        return pltpu.make_async_copy(zero_ref, xs_hbm.at[pl.ds(zs, SUB_ROWS), :], zsem)

    @pl.when(i == 0)
    def _():
        zero_ref[...] = jnp.zeros_like(zero_ref)

        def zstart(e, c):
            zero_copy(e).start()
            return c

        def zwait(e, c):
            zero_copy(e).wait()
            return c

        lax.fori_loop(0, n_experts, zstart, 0)
        lax.fori_loop(0, n_experts, zwait, 0)

    def start(g, c):
        for u in range(ROW_UNROLL):
            for k in range(TOP_K):
                dst = dest_ref[0, 0, (g * ROW_UNROLL + u) * TOP_K + k]
                pltpu.make_async_copy(x_ref.at[g, pl.ds(u, 1), :], xs_hbm.at[pl.ds(dst, 1), :],
                                      sem).start(priority=k % 2)
        return c

    lax.fori_loop(0, bt // ROW_UNROLL, start, 0)
    for k in range(TOP_K):
        pltpu.make_async_copy(x_ref, x_ref, sem).wait()


def _dispatch(zstart, dest3, x1p, *, rows_total, n_experts, bt=TOK_TILE):
    n, d = x1p.shape
    grid_spec = pltpu.PrefetchScalarGridSpec(
        num_scalar_prefetch=1,
        grid=(n // bt,),
        in_specs=[pl.BlockSpec((1, 1, bt * TOP_K), lambda i, zs: (i, 0, 0),
                               memory_space=pltpu.SMEM),
                  pl.BlockSpec((bt // ROW_UNROLL, ROW_UNROLL, d), lambda i, zs: (i, 0, 0))],
        out_specs=pl.BlockSpec(memory_space=pl.ANY),
        scratch_shapes=[pltpu.VMEM((SUB_ROWS, d), I32),
                        pltpu.SemaphoreType.DMA(()),
                        pltpu.SemaphoreType.DMA(())],
    )
    return pl.pallas_call(
        functools.partial(_dispatch_kernel, bt=bt, n_experts=n_experts),
        grid_spec=grid_spec,
        out_shape=jax.ShapeDtypeStruct((rows_total, d), I32),
        compiler_params=_cparams(("arbitrary",)),
        name="dispatch_rows",
    )(zstart, dest3, x1p.reshape(n // ROW_UNROLL, ROW_UNROLL, d))


def _experts_kernel(ce_ref, cv_ref, tot_ref, x_ref, w1_ref, w3_ref, w2_ref, o_ref,
                    w13_ref, acc_ref, h_ref, w2b_ref, *, n_up, fdim):
    c = pl.program_id(0)
    j = pl.program_id(1)
    nv = cv_ref[c]
    del ce_ref, tot_ref
    all_rows = slice(0, ROW_CHUNK)
    sub_blocks = [slice(sb * SUB_ROWS, (sb + 1) * SUB_ROWS) for sb in range(ROW_CHUNK // SUB_ROWS)]

    def up_rows(rows, first, last):
        xb = _unpack_bf16_pairs(x_ref[rows, :]).astype(BF16)
        part = jnp.dot(xb, w13_ref[...], preferred_element_type=F32)
        if not first:
            part = part + acc_ref[rows, :]
        if last:
            h1 = part[:, :fdim]
            h_ref[rows, :] = (h1 * jax.nn.sigmoid(h1) * part[:, fdim:]).astype(BF16)
        else:
            acc_ref[rows, :] = part

    def cast_up_weights():
        w13_ref[:, :fdim] = w1_ref[...].astype(BF16)
        w13_ref[:, fdim:] = w3_ref[...].astype(BF16)

    def up(first, last):
        @pl.when(nv == ROW_CHUNK)
        def _():
            cast_up_weights()
            up_rows(all_rows, first, last)

        @pl.when(nv < ROW_CHUNK)
        def _():
            cast_up_weights()
            for rows in sub_blocks:
                @pl.when(rows.start < nv)
                def _():
                    up_rows(rows, first, last)

    @pl.when(j == 0)
    def _():
        up(True, n_up == 1)

    if n_up > 2:
        @pl.when((j > 0) & (j < n_up - 1))
        def _():
            up(False, False)

    if n_up > 1:
        @pl.when(j == n_up - 1)
        def _():
            up(False, True)

    @pl.when(j >= n_up)
    def _():
        @pl.when(nv == ROW_CHUNK)
        def _():
            w2b_ref[...] = w2_ref[...].astype(BF16)
            o_ref[...] = _pack_tiles(
                jnp.dot(h_ref[...], w2b_ref[...], preferred_element_type=F32))

        @pl.when(nv < ROW_CHUNK)
        def _():
            w2b_ref[...] = w2_ref[...].astype(BF16)
            for rows in sub_blocks:
                @pl.when(rows.start < nv)
                def _():
                    o_ref[rows, :] = _pack_tiles(
                        jnp.dot(h_ref[rows, :], w2b_ref[...], preferred_element_type=F32))

                @pl.when(rows.start >= nv)
                def _():
                    o_ref[rows, :] = jnp.zeros((SUB_ROWS, o_ref.shape[1]), I32)


def _experts(chunk_expert, chunk_valid, total, xs, w1, w3, w2, *, n_chunks):
    d = w1.shape[1]
    fdim = w1.shape[2]
    n_up = d // UP_K_TILE
    n_down = d // DOWN_N_TILE

    def x_map(c, j, ce, cv, tot):
        return (jnp.minimum(c, tot[0] - 1), jnp.minimum(j, n_up - 1))

    def w13_map(c, j, ce, cv, tot):
        return (ce[c], jnp.minimum(j, n_up - 1), 0)

    def w2_map(c, j, ce, cv, tot):
        return (ce[c], 0, jnp.maximum(j - n_up, 0))

    def o_map(c, j, ce, cv, tot):
        return (c, jnp.maximum(j - n_up, 0))

    grid_spec = pltpu.PrefetchScalarGridSpec(
        num_scalar_prefetch=3,
        grid=(n_chunks, n_up + n_down),
        in_specs=[pl.BlockSpec((ROW_CHUNK, UP_K_TILE // 2), x_map),
                  pl.BlockSpec((None, UP_K_TILE, fdim), w13_map),
                  pl.BlockSpec((None, UP_K_TILE, fdim), w13_map),
                  pl.BlockSpec((None, fdim, DOWN_N_TILE), w2_map)],
        out_specs=pl.BlockSpec((ROW_CHUNK, DOWN_N_TILE // 2), o_map),
        scratch_shapes=[pltpu.VMEM((UP_K_TILE, 2 * fdim), BF16),
                        pltpu.VMEM((ROW_CHUNK, 2 * fdim), F32),
                        pltpu.VMEM((ROW_CHUNK, fdim), BF16),
                        pltpu.VMEM((fdim, DOWN_N_TILE), BF16)],
    )
    return pl.pallas_call(
        functools.partial(_experts_kernel, n_up=n_up, fdim=fdim),
        grid_spec=grid_spec,
        out_shape=jax.ShapeDtypeStruct((n_chunks * ROW_CHUNK, d // 2), I32),
        compiler_params=_cparams(("arbitrary", "arbitrary")),
        name="routed_experts",
    )(chunk_expert, chunk_valid, total, xs, w1, w3, w2)


def _shared_kernel(x_ref, w13_ref, w2_ref, o_ref, *, tf):
    f = pl.program_id(1)

    @pl.when(f == 0)
    def _():
        o_ref[...] = jnp.zeros_like(o_ref)

    rows_per = x_ref.shape[0] // ROW_GROUPS
    for g in range(ROW_GROUPS):
        rows = slice(g * rows_per, (g + 1) * rows_per)
        r = jnp.dot(x_ref[rows, :], w13_ref[...], preferred_element_type=F32)
        h1 = r[:, :tf]
        h = (h1 * jax.nn.sigmoid(h1) * r[:, tf:]).astype(BF16)
        o_ref[rows, :] += jnp.dot(h, w2_ref[...], preferred_element_type=F32)


def _shared_expert(x1b, w13s, w2s, *, bm=1024, tf=SHARED_F_TILE):
    n, d = x1b.shape
    nf = w2s.shape[0] // tf
    return pl.pallas_call(
        functools.partial(_shared_kernel, tf=tf),
        grid=(n // bm, nf),
        in_specs=[pl.BlockSpec((bm, d), lambda i, f: (i, 0)),
                  pl.BlockSpec((d, 2 * tf), lambda i, f: (0, f)),
                  pl.BlockSpec((tf, d), lambda i, f: (f, 0))],
        out_specs=pl.BlockSpec((bm, d), lambda i, f: (i, 0)),
        out_shape=jax.ShapeDtypeStruct((n, d), F32),
        compiler_params=_cparams(("parallel", "arbitrary")),
        name="shared_expert",
    )(x1b, w13s, w2s)


def _combine_kernel(dcur_ref, dnext_ref, x1_ref, ysh_ref, gate_ref, g_ref, b_ref, ys_hbm, o_ref,
                    buf_ref, sem, *, bt):
    i = pl.program_id(0)
    slot = lax.rem(i, 2)

    def issue(dref, s):
        def start(g, c):
            for u in range(ROW_UNROLL):
                for k in range(TOP_K):
                    src = dref[0, 0, (g * ROW_UNROLL + u) * TOP_K + k]
                    pltpu.make_async_copy(ys_hbm.at[pl.ds(src, 1), :],
                                          buf_ref.at[s, k, g, pl.ds(u, 1), :],
                                          sem.at[s]).start(priority=k % 2)
            return c

        lax.fori_loop(0, bt // ROW_UNROLL, start, 0)

    @pl.when(i == 0)
    def _():
        issue(dcur_ref, slot)

    @pl.when(i + 1 < pl.num_programs(0))
    def _():
        issue(dnext_ref, 1 - slot)

    acc = ALPHA * x1_ref[...] + ysh_ref[...]

    for k in range(TOP_K):
        pltpu.make_async_copy(buf_ref.at[slot, k], buf_ref.at[slot, k], sem.at[slot]).wait()
    gate = gate_ref[...]
    half = PACK_TILE // 2
    for k in range(TOP_K):
        yk = jnp.concatenate(
            [_unpack_bf16_pairs(buf_ref[slot, k, :, :, t * half:(t + 1) * half].reshape(bt, half))
             for t in range(acc.shape[1] // PACK_TILE)], axis=1)
        acc = acc + gate[:, k:k + 1] * yk
    o_ref[...] = _layer_norm(acc, g_ref[...], b_ref[...])


def _combine(dest3, x1, ysh, gate, g2, b2, ys, *, bt=TOK_TILE):
    n, d = x1.shape
    nt = n // bt
    row = lambda i: (i, 0)
    fix = lambda i: (0, 0)
    return pl.pallas_call(
        functools.partial(_combine_kernel, bt=bt),
        grid=(nt,),
        in_specs=[pl.BlockSpec((1, 1, bt * TOP_K), lambda i: (i, 0, 0), memory_space=pltpu.SMEM),
                  pl.BlockSpec((1, 1, bt * TOP_K), lambda i: (jnp.minimum(i + 1, nt - 1), 0, 0),
                               memory_space=pltpu.SMEM),
                  pl.BlockSpec((bt, d), row), pl.BlockSpec((bt, d), row),
                  pl.BlockSpec((bt, LANES), row),
                  pl.BlockSpec((1, d), fix), pl.BlockSpec((1, d), fix),
                  pl.BlockSpec(memory_space=pl.ANY)],
        out_specs=pl.BlockSpec((bt, d), row),
        out_shape=jax.ShapeDtypeStruct((n, d), F32),
        scratch_shapes=[pltpu.VMEM((2, TOP_K, bt // ROW_UNROLL, ROW_UNROLL, d // 2), I32),
                        pltpu.SemaphoreType.DMA((2,))],
        compiler_params=_cparams(("arbitrary",)),
        name="combine_ln2",
    )(dest3, dest3, x1, ysh, gate, g2, b2, ys)


def _layer(x, w_in, b_forget, pool_w, pool_scale, w_branch_pool, w_branch_attn, w_out,
           ln1_g, ln1_b, w_router, router_bias, w1, w3, w2, w_shared1, w_shared3, w_shared2,
           ln2_g, ln2_b):
    batch, seq, d = x.shape
    n = batch * seq
    pool_width = w_branch_pool.shape[0]
    attn_width = w_branch_attn.shape[0]
    n_heads = attn_width // HEAD_DIM
    n_experts = w1.shape[0]
    off_q = pool_width
    off_k = off_q + attn_width
    off_v = off_k + attn_width
    off_f = off_v + attn_width
    off_gate = off_f + n_heads

    x2 = x.reshape(n, d)

    w_main = w_in[:, :off_f].astype(BF16)
    wf_pad = jnp.pad(w_in[:, off_f:off_gate], ((0, 0), (0, LANES - n_heads))).astype(BF16)
    bf_pad = jnp.pad(b_forget.astype(F32), (0, LANES - n_heads)).reshape(1, LANES)
    w_gate = w_in[:, off_gate:].astype(BF16)
    piece = jnp.arange(3)[:, None, None]
    src = jnp.arange(LANES)[None, :, None]
    dst = jnp.arange(n_heads * LANES)[None, None, :]
    sel = ((src < n_heads) & (dst == src * LANES + piece)).astype(BF16)
    wr = jnp.pad(w_router.astype(F32), ((0, 0), (0, LANES - n_experts)))
    wr_hi = wr.astype(BF16)
    wr_lo = (wr - wr_hi.astype(F32)).astype(BF16)
    rb_pad = jnp.pad(router_bias.astype(F32), (0, LANES - n_experts)).reshape(1, LANES)
    tf = SHARED_F_TILE
    fs = w_shared1.shape[1]
    w13s = jnp.concatenate([w_shared1.reshape(d, fs // tf, tf), w_shared3.reshape(d, fs // tf, tf)],
                           axis=2).reshape(d, 2 * fs).astype(BF16)

    z = _inproj(x2, w_main, q_lo=off_q, q_hi=off_k, q_scale=HEAD_DIM ** -0.5)
    ka, edge = _forget_bias(x2, wf_pad, bf_pad, sel, batch=batch, seq=seq, n_heads=n_heads)
    n_kblk = seq // ATTN_BLOCK
    edge = edge.reshape(batch, n_kblk, 8, LANES)[:, :, :1, :n_heads]
    blast = edge[:, :, 0, :].transpose(0, 2, 1).reshape(-1)
    pool_o = _pool(z, pool_w.astype(BF16), pool_scale.astype(F32).reshape(1, pool_width),
                   batch=batch, seq=seq, width=pool_width)
    attn_o = _attention(blast, z, ka, batch=batch, seq=seq, n_heads=n_heads,
                        q_col=off_q // HEAD_DIM, k_col=off_k // HEAD_DIM, v_col=off_v // HEAD_DIM)
    merged = _merge(x2, pool_o, attn_o, w_gate, w_branch_pool.astype(BF16),
                    w_branch_attn.astype(BF16))
    x1, x1b, x1p, logits = _out_ln_router(x2, merged, w_out.astype(BF16),
                                     ln1_g.astype(F32).reshape(1, d), ln1_b.astype(F32).reshape(1, d),
                                     wr_hi, wr_lo)

    idx, gate, rank, cnt = _route(logits, rb_pad, n_experts=n_experts)
    counts = cnt[0, :n_experts]
    n_assign = n * TOP_K
    n_chunks = -(-n_assign // ROW_CHUNK) + n_experts
    chunks_per = (counts + ROW_CHUNK - 1) // ROW_CHUNK
    chunk_end = jnp.cumsum(chunks_per)
    chunk_begin = chunk_end - chunks_per
    total = chunk_end[-1]
    group_start = chunk_begin * ROW_CHUNK
    cids = jnp.arange(n_chunks, dtype=I32)
    c_eff = jnp.minimum(cids, total - 1)
    chunk_expert = jnp.minimum(jnp.sum((chunk_end[None, :] <= c_eff[:, None]).astype(I32), axis=1),
                               n_experts - 1).astype(I32)
    chunk_valid = jnp.where(cids < total,
                            jnp.minimum(counts[chunk_expert] - (cids - chunk_begin[chunk_expert]) * ROW_CHUNK,
                                        ROW_CHUNK), 0).astype(I32)
    idx_k = idx[:, :TOP_K]
    dest = group_start[idx_k].astype(I32) + rank[:, :TOP_K]
    dest3 = dest.reshape(n // TOK_TILE, 1, TOK_TILE * TOP_K)
    zstart = ((group_start + counts) // SUB_ROWS * SUB_ROWS).astype(I32)
    rows_total = (n_chunks + 1) * ROW_CHUNK

    xs = _dispatch(zstart, dest3, x1p, rows_total=rows_total, n_experts=n_experts)
    ys = _experts(chunk_expert, chunk_valid, total.reshape(1).astype(I32), xs, w1, w3, w2,
                  n_chunks=n_chunks)
    ysh = _shared_expert(x1b, w13s, w_shared2.astype(BF16))
    out = _combine(dest3, x1, ysh, gate, ln2_g.astype(F32).reshape(1, d),
                   ln2_b.astype(F32).reshape(1, d), ys)
    return out.reshape(batch, seq, d)


def kernel(x, w_in, b_forget, pool_w, pool_scale, w_branch_pool, w_branch_attn, w_out, ln1_g, ln1_b,
           w_router, router_bias, w1, w3, w2, w_shared1, w_shared3, w_shared2, ln2_g, ln2_b):
    for l in range(w_in.shape[0]):
        x = _layer(x, w_in[l], b_forget[l], pool_w[l], pool_scale[l], w_branch_pool[l],
                   w_branch_attn[l], w_out[l], ln1_g[l], ln1_b[l], w_router[l], router_bias[l],
                   w1[l], w3[l], w2[l], w_shared1[l], w_shared3[l], w_shared2[l], ln2_g[l], ln2_b[l])
    return x
```

```python
import functools

import jax
import jax.numpy as jnp
from jax import lax
from jax.experimental import pallas as pl
from jax.experimental.pallas import tpu as pltpu

F32 = jnp.float32
BF16 = jnp.bfloat16
I32 = jnp.int32

POOL_WINDOWS = (2, 4, 8, 16)
HEAD_DIM = 128
TOP_K = 6
ROUTE_SCALE = 1.0
DEPTH = 1
ALPHA = (2 * DEPTH) ** 0.25
LN_EPS = 1e-5

LANES = 128
MAX_WINDOW = max(POOL_WINDOWS)
ATTN_BLOCK = 512
Q_HALVES = 2
SKIP_MARGIN = 110.0
ROW_CHUNK = 1024
SUB_ROWS = 256
PACK_TILE = 512
UP_K_TILE = PACK_TILE
DOWN_N_TILE = 2 * PACK_TILE
SHARED_F_TILE = 256
TOK_TILE = 256
ROW_UNROLL = 8
ROW_GROUPS = 2
VMEM_LIMIT = 56 * 1024 * 1024


def _cparams(sem, vmem=VMEM_LIMIT):
    return pltpu.CompilerParams(dimension_semantics=sem, vmem_limit_bytes=vmem)


def _split3(v):
    hi = v.astype(BF16)
    r1 = v - hi.astype(F32)
    mid = r1.astype(BF16)
    lo = (r1 - mid.astype(F32)).astype(BF16)
    return hi, mid, lo


def _pack_bf16_pairs(v):
    c = v.shape[1] // 2
    lo = lax.bitcast_convert_type(v[:, :c].astype(BF16).astype(F32), I32)
    hi = lax.bitcast_convert_type(v[:, c:].astype(BF16).astype(F32), I32)
    return ((lo >> 16) & 0xFFFF) | (hi & -65536)


def _unpack_bf16_pairs(w):
    lo = lax.bitcast_convert_type(w << 16, F32)
    hi = lax.bitcast_convert_type(w & -65536, F32)
    return jnp.concatenate([lo, hi], axis=1)


def _pack_tiles(v):
    return jnp.concatenate([_pack_bf16_pairs(v[:, g * PACK_TILE:(g + 1) * PACK_TILE])
                            for g in range(v.shape[1] // PACK_TILE)], axis=1)


def _layer_norm(v, g, b):
    mu = jnp.mean(v, axis=-1, keepdims=True)
    d = v - mu
    var = jnp.mean(d * d, axis=-1, keepdims=True)
    return d * lax.rsqrt(var + LN_EPS) * g + b


def _inproj_kernel(x_ref, w_ref, o_ref, xb_ref, *, bn, q_lo, q_hi, q_scale):
    j = pl.program_id(1)

    @pl.when(j == 0)
    def _():
        xb_ref[...] = x_ref[...].astype(BF16)

    acc = jnp.dot(xb_ref[...], w_ref[...], preferred_element_type=F32)
    col0 = j * bn
    s = jnp.where((col0 >= q_lo) & (col0 < q_hi), q_scale, 1.0).astype(F32)
    o_ref[...] = (acc * s).astype(BF16)


def _inproj(x2, w_main, *, q_lo, q_hi, q_scale, bm=1024, bn=1024):
    n, d = x2.shape
    c = w_main.shape[1]
    return pl.pallas_call(
        functools.partial(_inproj_kernel, bn=bn, q_lo=q_lo, q_hi=q_hi, q_scale=q_scale),
        grid=(n // bm, c // bn),
        in_specs=[pl.BlockSpec((bm, d), lambda i, j: (i, 0)),
                  pl.BlockSpec((d, bn), lambda i, j: (0, j))],
        out_specs=pl.BlockSpec((bm, bn), lambda i, j: (i, j)),
        out_shape=jax.ShapeDtypeStruct((n, c), BF16),
        scratch_shapes=[pltpu.VMEM((bm, d), BF16)],
        compiler_params=_cparams(("parallel", "arbitrary")),
        name="inproj",
    )(x2, w_main)


def _forget_kernel(x_ref, wf_ref, b_ref, sel_ref, ka_ref, edge_ref, carry_ref, *, ts, n_heads):
    i = pl.program_id(1)

    @pl.when(i == 0)
    def _():
        carry_ref[...] = jnp.zeros_like(carry_ref)

    z = jnp.dot(x_ref[...].astype(BF16), wf_ref[...], preferred_element_type=F32) + b_ref[...]
    lf = jnp.minimum(z, 0.0) - jnp.log1p(jnp.exp(-jnp.abs(z)))
    lane = lax.broadcasted_iota(I32, lf.shape, 1)
    lf = jnp.where(lane < n_heads, lf, 0.0)

    row = lax.broadcasted_iota(I32, (ts, ts), 0)
    col = lax.broadcasted_iota(I32, (ts, ts), 1)
    tri = jnp.where(col <= row, 1.0, 0.0).astype(BF16)
    hi, mid, lo = _split3(lf)
    cs = (jnp.dot(tri, hi, preferred_element_type=F32)
          + jnp.dot(tri, mid, preferred_element_type=F32)
          + jnp.dot(tri, lo, preferred_element_type=F32)) + carry_ref[...]
    carry_ref[...] = cs[ts - 1:ts, :]
    edge_ref[0] = jnp.broadcast_to(-cs[ts - 1:ts, :], (8, LANES))

    h2, m2, l2 = _split3(-cs)
    ka = (jnp.dot(h2, sel_ref[0], preferred_element_type=F32)
          + jnp.dot(m2, sel_ref[1], preferred_element_type=F32)
          + jnp.dot(l2, sel_ref[2], preferred_element_type=F32))
    ka_ref[...] = ka.astype(BF16)


def _forget_bias(x2, wf_pad, b_pad, sel, *, batch, seq, n_heads, ts=ATTN_BLOCK):
    n, d = x2.shape
    nt = seq // ts
    return pl.pallas_call(
        functools.partial(_forget_kernel, ts=ts, n_heads=n_heads),
        grid=(batch, nt),
        in_specs=[pl.BlockSpec((ts, d), lambda b, i: (b * nt + i, 0)),
                  pl.BlockSpec((d, LANES), lambda b, i: (0, 0)),
                  pl.BlockSpec((1, LANES), lambda b, i: (0, 0)),
                  pl.BlockSpec((3, LANES, n_heads * LANES), lambda b, i: (0, 0, 0))],
        out_specs=[pl.BlockSpec((ts, n_heads * LANES), lambda b, i: (b * nt + i, 0)),
                   pl.BlockSpec((1, 8, LANES), lambda b, i: (b * nt + i, 0, 0))],
        out_shape=[jax.ShapeDtypeStruct((n, n_heads * LANES), BF16),
                   jax.ShapeDtypeStruct((batch * nt, 8, LANES), F32)],
        scratch_shapes=[pltpu.VMEM((1, LANES), F32)],
        compiler_params=_cparams(("arbitrary", "arbitrary")),
        name="forget_bias",
    )(x2, wf_pad, b_pad, sel)


def _pool_kernel(u_ref, halo_ref, pw_ref, sc_ref, o_ref, buf_ref, *, ts, gdim):
    i = pl.program_id(1)
    h = MAX_WINDOW
    halo = halo_ref[...].astype(F32)
    buf_ref[0:h, :] = jnp.where(i > 0, halo, 0.0)
    buf_ref[h:h + ts, :] = u_ref[...].astype(F32)

    t = i * ts + lax.broadcasted_iota(I32, (ts, 1), 0)
    for g, w in enumerate(POOL_WINDOWS):
        cs = slice(g * gdim, (g + 1) * gdim)
        tok = buf_ref[h:h + ts, cs]
        wsum = tok
        for k in range(1, w):
            wsum = wsum + buf_ref[h - k:h - k + ts, cs]
        cnt = jnp.minimum(t + 1, w).astype(F32)
        p = (wsum / cnt - tok).astype(BF16)
        y = jnp.dot(p, pw_ref[g], preferred_element_type=F32) * sc_ref[:, cs]
        o_ref[:, cs] = y.astype(BF16)


def _pool(z, pool_w_b, pool_scale2, *, batch, seq, width, ts=512):
    n = z.shape[0]
    groups = len(POOL_WINDOWS)
    gdim = width // groups
    nt = seq // ts
    hb = ts // MAX_WINDOW
    return pl.pallas_call(
        functools.partial(_pool_kernel, ts=ts, gdim=gdim),
        grid=(batch, nt),
        in_specs=[pl.BlockSpec((ts, width), lambda b, i: (b * nt + i, 0)),
                  pl.BlockSpec((MAX_WINDOW, width),
                               lambda b, i: (jnp.maximum((b * nt + i) * hb - 1, 0), 0)),
                  pl.BlockSpec((groups, gdim, gdim), lambda b, i: (0, 0, 0)),
                  pl.BlockSpec((1, width), lambda b, i: (0, 0))],
        out_specs=pl.BlockSpec((ts, width), lambda b, i: (b * nt + i, 0)),
        out_shape=jax.ShapeDtypeStruct((n, width), BF16),
        scratch_shapes=[pltpu.VMEM((ts + MAX_WINDOW, width), F32)],
        compiler_params=_cparams(("parallel", "parallel")),
        name="pool_mixer",
    )(z, z, pool_w_b, pool_scale2)


def _attn_kernel(blast_ref, q_ref, k_ref, ka_ref, v_ref, o_ref,
                 qa_ref, m_ref, l_ref, acc_ref, kmax_ref, *, blk, n_kblk):
    qi = pl.program_id(2)
    base = (pl.program_id(0) * pl.num_programs(1) + pl.program_id(1)) * n_kblk

    @pl.when(qi == 0)
    def _():
        def knorm(c, run):
            kk = k_ref[pl.ds(pl.multiple_of(c * blk, blk), blk), :].astype(F32)
            return jnp.maximum(run, jnp.sum(kk * kk, axis=1, keepdims=True))

        run = lax.fori_loop(0, n_kblk, knorm, jnp.zeros((blk, 1), F32))
        kmax_ref[0] = jnp.max(jnp.sqrt(run))

    lane = lax.broadcasted_iota(I32, (blk, HEAD_DIM), 1)
    ones3 = jnp.where(lane < 3, 1.0, 0.0).astype(BF16)
    for h in range(Q_HALVES):
        qa_ref[h] = jnp.concatenate([q_ref[h * blk:(h + 1) * blk, :], ones3], axis=1)
    m_ref[...] = jnp.full_like(m_ref, -jnp.inf)
    l_ref[...] = jnp.zeros_like(l_ref)
    acc_ref[...] = jnp.zeros_like(acc_ref)
    reps = blk // LANES

    def step(h, ki, masked):
        ks = pl.multiple_of(ki * blk, blk)
        k_aug = jnp.concatenate([k_ref[pl.ds(ks, blk), :], ka_ref[pl.ds(ks, blk), :]], axis=1)
        s = lax.dot_general(qa_ref[h], k_aug, (((1,), (1,)), ((), ())),
                            preferred_element_type=F32)
        if masked:
            r = lax.broadcasted_iota(I32, (blk, blk), 0)
            c = lax.broadcasted_iota(I32, (blk, blk), 1)
            s = jnp.where(c <= r, s, -jnp.inf)
        m_prev = m_ref[h]
        m_next = jnp.maximum(m_prev, jnp.max(s, axis=1, keepdims=True))
        alpha = jnp.exp(m_prev - m_next)
        p = jnp.exp(s - jnp.concatenate([m_next] * reps, axis=1))
        l_ref[h] = alpha * l_ref[h] + jnp.sum(p, axis=1, keepdims=True)
        acc_ref[h] = alpha * acc_ref[h] + jnp.dot(
            p.astype(BF16), v_ref[pl.ds(ks, blk), :], preferred_element_type=F32)
        m_ref[h] = m_next

    def body(kq, carry):
        for d in range(Q_HALVES):
            for h in range(Q_HALVES):
                step(h, Q_HALVES * kq + d, False)
        return carry

    for h in range(Q_HALVES):
        for d in range(h):
            step(h, Q_HALVES * qi + d, False)
        step(h, Q_HALVES * qi + h, True)

    qq = q_ref[...].astype(F32)
    qmax = jnp.max(jnp.sqrt(jnp.sum(qq * qq, axis=1, keepdims=True)))
    thr = jnp.min(m_ref[...]) - SKIP_MARGIN - qmax * kmax_ref[0]

    def count_skippable(j, a):
        return a + jnp.where(blast_ref[base + j] < thr, 1, 0).astype(I32)

    n_skip = lax.fori_loop(0, Q_HALVES * qi, count_skippable, jnp.int32(0))
    lax.fori_loop(n_skip // Q_HALVES, qi, body, 0)
    for h in range(Q_HALVES):
        o_ref[h * blk:(h + 1) * blk, :] = (acc_ref[h] / l_ref[h]).astype(BF16)


def _attention(blast, z, ka, *, batch, seq, n_heads, q_col, k_col, v_col, blk=ATTN_BLOCK):
    n = z.shape[0]
    qb = Q_HALVES * blk
    nq = seq // qb
    grid_spec = pltpu.PrefetchScalarGridSpec(
        num_scalar_prefetch=1,
        grid=(batch, n_heads, nq),
        in_specs=[pl.BlockSpec((qb, HEAD_DIM), lambda b, h, i, bl: (b * nq + i, q_col + h)),
                  pl.BlockSpec((seq, HEAD_DIM), lambda b, h, i, bl: (b, k_col + h)),
                  pl.BlockSpec((seq, HEAD_DIM), lambda b, h, i, bl: (b, h)),
                  pl.BlockSpec((seq, HEAD_DIM), lambda b, h, i, bl: (b, v_col + h))],
        out_specs=pl.BlockSpec((qb, HEAD_DIM), lambda b, h, i, bl: (b * nq + i, h)),
        scratch_shapes=[pltpu.VMEM((Q_HALVES, blk, 2 * HEAD_DIM), BF16),
                        pltpu.VMEM((Q_HALVES, blk, LANES), F32),
                        pltpu.VMEM((Q_HALVES, blk, LANES), F32),
                        pltpu.VMEM((Q_HALVES, blk, HEAD_DIM), F32),
                        pltpu.SMEM((1,), F32)],
    )
    return pl.pallas_call(
        functools.partial(_attn_kernel, blk=blk, n_kblk=seq // blk),
        grid_spec=grid_spec,
        out_shape=jax.ShapeDtypeStruct((n, n_heads * HEAD_DIM), BF16),
        compiler_params=_cparams(("arbitrary", "arbitrary", "arbitrary")),
        name="fox_attention",
    )(blast, z, z, ka, z)


def _merge_kernel(x_ref, p_ref, a_ref, wg0_ref, wg1_ref, wbp_ref, wba_ref, o_ref, xb_ref):
    j = pl.program_id(1)

    @pl.when(j == 0)
    def _():
        xb_ref[...] = x_ref[...].astype(BF16)

    xb = xb_ref[...]
    g0 = jax.nn.sigmoid(jnp.dot(xb, wg0_ref[...], preferred_element_type=F32))
    g1 = jax.nn.sigmoid(jnp.dot(xb, wg1_ref[...], preferred_element_type=F32))
    yp = jnp.dot(p_ref[...], wbp_ref[...], preferred_element_type=F32)
    ya = jnp.dot(a_ref[...], wba_ref[...], preferred_element_type=F32)
    o_ref[...] = (g0 * yp + g1 * ya).astype(BF16)


def _merge(x2, pool_o, attn_o, wg, wbp, wba, *, bm=1024, bn=512):
    n, d = x2.shape
    wp = pool_o.shape[1]
    wa = attn_o.shape[1]
    nj = d // bn
    return pl.pallas_call(
        _merge_kernel,
        grid=(n // bm, nj),
        in_specs=[pl.BlockSpec((bm, d), lambda i, j: (i, 0)),
                  pl.BlockSpec((bm, wp), lambda i, j: (i, 0)),
                  pl.BlockSpec((bm, wa), lambda i, j: (i, 0)),
                  pl.BlockSpec((d, bn), lambda i, j: (0, j)),
                  pl.BlockSpec((d, bn), lambda i, j: (0, nj + j)),
                  pl.BlockSpec((wp, bn), lambda i, j: (0, j)),
                  pl.BlockSpec((wa, bn), lambda i, j: (0, j))],
        out_specs=pl.BlockSpec((bm, bn), lambda i, j: (i, j)),
        out_shape=jax.ShapeDtypeStruct((n, d), BF16),
        scratch_shapes=[pltpu.VMEM((bm, d), BF16)],
        compiler_params=_cparams(("parallel", "arbitrary")),
        name="branch_merge",
    )(x2, pool_o, attn_o, wg, wg, wbp, wba)


def _out_ln_router_kernel(x_ref, m_ref, wo_ref, g_ref, b_ref, wrh_ref, wrl_ref,
                          x1_ref, x1b_ref, x1p_ref, lg_ref):
    bm = x_ref.shape[0]
    rows_per = bm // ROW_GROUPS
    half = PACK_TILE // 2
    for r in range(ROW_GROUPS):
        rows = slice(r * rows_per, (r + 1) * rows_per)
        mix = jnp.dot(m_ref[rows, :], wo_ref[...], preferred_element_type=F32)
        x1 = _layer_norm(ALPHA * x_ref[rows, :] + mix, g_ref[...], b_ref[...])
        x1_ref[rows, :] = x1
        hi = x1.astype(BF16)
        x1b_ref[rows, :] = hi
        for t in range(x1.shape[1] // PACK_TILE):
            x1p_ref[rows, t * half:(t + 1) * half] = _pack_bf16_pairs(
                x1[:, t * PACK_TILE:(t + 1) * PACK_TILE])
        lo = (x1 - hi.astype(F32)).astype(BF16)
        lg_ref[rows, :] = (jnp.dot(hi, wrh_ref[...], preferred_element_type=F32)
                           + jnp.dot(lo, wrh_ref[...], preferred_element_type=F32)
                           + jnp.dot(hi, wrl_ref[...], preferred_element_type=F32))


def _out_ln_router(x2, merged, wo, g1, b1, wr_hi, wr_lo, *, bm=512):
    n, d = x2.shape
    row = lambda i: (i, 0)
    fix = lambda i: (0, 0)
    return pl.pallas_call(
        _out_ln_router_kernel,
        grid=(n // bm,),
        in_specs=[pl.BlockSpec((bm, d), row), pl.BlockSpec((bm, d), row),
                  pl.BlockSpec((d, d), fix), pl.BlockSpec((1, d), fix), pl.BlockSpec((1, d), fix),
                  pl.BlockSpec((d, LANES), fix), pl.BlockSpec((d, LANES), fix)],
        out_specs=[pl.BlockSpec((bm, d), row), pl.BlockSpec((bm, d), row),
                   pl.BlockSpec((bm, d // 2), row), pl.BlockSpec((bm, LANES), row)],
        out_shape=[jax.ShapeDtypeStruct((n, d), F32), jax.ShapeDtypeStruct((n, d), BF16),
                   jax.ShapeDtypeStruct((n, d // 2), I32), jax.ShapeDtypeStruct((n, LANES), F32)],
        compiler_params=_cparams(("parallel",)),
        name="outproj_ln1_router",
    )(x2, merged, wo, g1, b1, wr_hi, wr_lo)


def _route_kernel(lg_ref, bias_ref, idx_ref, gate_ref, rank_ref, cnt_ref, carry_ref,
                  *, bm, n_experts):
    i = pl.program_id(0)

    @pl.when(i == 0)
    def _():
        carry_ref[...] = jnp.zeros_like(carry_ref)

    lane = lax.broadcasted_iota(I32, (bm, LANES), 1)
    lane_f = lane.astype(F32)
    scores = jax.nn.sigmoid(lg_ref[...])
    sel = jnp.where(lane < n_experts, scores + bias_ref[...], -jnp.inf)

    hits, gates = [], []
    gsum = jnp.zeros((bm, 1), F32)
    member = jnp.zeros((bm, LANES), F32)
    idx_out = jnp.zeros((bm, LANES), F32)
    for k in range(TOP_K):
        m = jnp.max(sel, axis=1, keepdims=True)
        ik = jnp.min(jnp.where(sel == m, lane_f, float(LANES)), axis=1, keepdims=True)
        hit = lane_f == ik
        gk = jnp.sum(jnp.where(hit, scores, 0.0), axis=1, keepdims=True)
        sel = jnp.where(hit, -jnp.inf, sel)
        member = jnp.where(hit, 1.0, member)
        idx_out = jnp.where(lane == k, ik, idx_out)
        gsum = gsum + gk
        hits.append(hit)
        gates.append(gk)

    row = lax.broadcasted_iota(I32, (bm, bm), 0)
    col = lax.broadcasted_iota(I32, (bm, bm), 1)
    before = jnp.where(col < row, 1.0, 0.0).astype(BF16)
    prefix = jnp.dot(before, member.astype(BF16), preferred_element_type=F32) + carry_ref[...]

    gate_out = jnp.zeros((bm, LANES), F32)
    rank_out = jnp.zeros((bm, LANES), F32)
    for k in range(TOP_K):
        rk = jnp.sum(jnp.where(hits[k], prefix, 0.0), axis=1, keepdims=True)
        rank_out = jnp.where(lane == k, rk, rank_out)
        gate_out = jnp.where(lane == k, gates[k] / gsum * ROUTE_SCALE, gate_out)

    total = carry_ref[...] + jnp.sum(member, axis=0, keepdims=True)
    carry_ref[...] = total
    idx_ref[...] = idx_out.astype(I32)
    gate_ref[...] = gate_out
    rank_ref[...] = rank_out.astype(I32)
    cnt_ref[...] = jnp.broadcast_to(total, cnt_ref.shape).astype(I32)


def _route(logits, bias_pad, *, n_experts, bm=2 * TOK_TILE):
    n = logits.shape[0]
    row = lambda i: (i, 0)
    fix = lambda i: (0, 0)
    return pl.pallas_call(
        functools.partial(_route_kernel, bm=bm, n_experts=n_experts),
        grid=(n // bm,),
        in_specs=[pl.BlockSpec((bm, LANES), row), pl.BlockSpec((1, LANES), fix)],
        out_specs=[pl.BlockSpec((bm, LANES), row), pl.BlockSpec((bm, LANES), row),
                   pl.BlockSpec((bm, LANES), row), pl.BlockSpec((8, LANES), fix)],
        out_shape=[jax.ShapeDtypeStruct((n, LANES), I32), jax.ShapeDtypeStruct((n, LANES), F32),
                   jax.ShapeDtypeStruct((n, LANES), I32), jax.ShapeDtypeStruct((8, LANES), I32)],
        scratch_shapes=[pltpu.VMEM((1, LANES), F32)],
        compiler_params=_cparams(("arbitrary",)),
        name="route_topk",
    )(logits, bias_pad)


def _dispatch_kernel(zs_ref, dest_ref, x_ref, xs_hbm, zero_ref, sem, zsem, *, bt, n_experts):
    i = pl.program_id(0)

    def zero_copy(e):
        zs = pl.multiple_of(zs_ref[e], SUB_ROWS)
        return pltpu.make_async_copy(zero_ref, xs_hbm.at[pl.ds(zs, SUB_ROWS), :], zsem)

    @pl.when(i == 0)
    def _():
        zero_ref[...] = jnp.zeros_like(zero_ref)

        def zstart(e, c):
            zero_copy(e).start()
            return c

        def zwait(e, c):
            zero_copy(e).wait()
            return c

        lax.fori_loop(0, n_experts, zstart, 0)
        lax.fori_loop(0, n_experts, zwait, 0)

    def start(g, c):
        for u in range(ROW_UNROLL):
            for k in range(TOP_K):
                dst = dest_ref[0, 0, (g * ROW_UNROLL + u) * TOP_K + k]
                pltpu.make_async_copy(x_ref.at[g, pl.ds(u, 1), :], xs_hbm.at[pl.ds(dst, 1), :],
                                      sem).start(priority=k % 2)
        return c

    lax.fori_loop(0, bt // ROW_UNROLL, start, 0)
    for k in range(TOP_K):
        pltpu.make_async_copy(x_ref, x_ref, sem).wait()


def _dispatch(zstart, dest3, x1p, *, rows_total, n_experts, bt=TOK_TILE):
    n, d = x1p.shape
    grid_spec = pltpu.PrefetchScalarGridSpec(
        num_scalar_prefetch=1,
        grid=(n // bt,),
        in_specs=[pl.BlockSpec((1, 1, bt * TOP_K), lambda i, zs: (i, 0, 0),
                               memory_space=pltpu.SMEM),
                  pl.BlockSpec((bt // ROW_UNROLL, ROW_UNROLL, d), lambda i, zs: (i, 0, 0))],
        out_specs=pl.BlockSpec(memory_space=pl.ANY),
        scratch_shapes=[pltpu.VMEM((SUB_ROWS, d), I32),
                        pltpu.SemaphoreType.DMA(()),
                        pltpu.SemaphoreType.DMA(())],
    )
    return pl.pallas_call(
        functools.partial(_dispatch_kernel, bt=bt, n_experts=n_experts),
        grid_spec=grid_spec,
        out_shape=jax.ShapeDtypeStruct((rows_total, d), I32),
        compiler_params=_cparams(("arbitrary",)),
        name="dispatch_rows",
    )(zstart, dest3, x1p.reshape(n // ROW_UNROLL, ROW_UNROLL, d))


def _experts_kernel(ce_ref, cv_ref, tot_ref, x_ref, w1_ref, w3_ref, w2_ref, o_ref,
                    w13_ref, acc_ref, h_ref, w2b_ref, *, n_up, fdim):
    c = pl.program_id(0)
    j = pl.program_id(1)
    nv = cv_ref[c]
    del ce_ref, tot_ref
    all_rows = slice(0, ROW_CHUNK)
    sub_blocks = [slice(sb * SUB_ROWS, (sb + 1) * SUB_ROWS) for sb in range(ROW_CHUNK // SUB_ROWS)]

    def up_rows(rows, first, last):
        xb = _unpack_bf16_pairs(x_ref[rows, :]).astype(BF16)
        part = jnp.dot(xb, w13_ref[...], preferred_element_type=F32)
        if not first:
            part = part + acc_ref[rows, :]
        if last:
            h1 = part[:, :fdim]
            h_ref[rows, :] = (h1 * jax.nn.sigmoid(h1) * part[:, fdim:]).astype(BF16)
        else:
            acc_ref[rows, :] = part

    def cast_up_weights():
        w13_ref[:, :fdim] = w1_ref[...].astype(BF16)
        w13_ref[:, fdim:] = w3_ref[...].astype(BF16)

    def up(first, last):
        @pl.when(nv > ROW_CHUNK - SUB_ROWS)
        def _():
            cast_up_weights()
            up_rows(all_rows, first, last)

        @pl.when(nv <= ROW_CHUNK - SUB_ROWS)
        def _():
            cast_up_weights()
            for rows in sub_blocks:
                @pl.when(rows.start < nv)
                def _():
                    up_rows(rows, first, last)

    @pl.when(j == 0)
    def _():
        up(True, n_up == 1)

    if n_up > 2:
        @pl.when((j > 0) & (j < n_up - 1))
        def _():
            up(False, False)

    if n_up > 1:
        @pl.when(j == n_up - 1)
        def _():
            up(False, True)

    @pl.when(j >= n_up)
    def _():
        @pl.when(nv > ROW_CHUNK - SUB_ROWS)
        def _():
            w2b_ref[...] = w2_ref[...].astype(BF16)
            o_ref[...] = _pack_tiles(
                jnp.dot(h_ref[...], w2b_ref[...], preferred_element_type=F32))

        @pl.when(nv <= ROW_CHUNK - SUB_ROWS)
        def _():
            w2b_ref[...] = w2_ref[...].astype(BF16)
            for rows in sub_blocks:
                @pl.when(rows.start < nv)
                def _():
                    o_ref[rows, :] = _pack_tiles(
                        jnp.dot(h_ref[rows, :], w2b_ref[...], preferred_element_type=F32))

                @pl.when(rows.start >= nv)
                def _():
                    o_ref[rows, :] = jnp.zeros((SUB_ROWS, o_ref.shape[1]), I32)


def _experts(chunk_expert, chunk_valid, total, xs, w1, w3, w2, *, n_chunks):
    d = w1.shape[1]
    fdim = w1.shape[2]
    n_up = d // UP_K_TILE
    n_down = d // DOWN_N_TILE

    def x_map(c, j, ce, cv, tot):
        return (jnp.minimum(c, tot[0] - 1), jnp.minimum(j, n_up - 1))

    def w13_map(c, j, ce, cv, tot):
        return (ce[c], jnp.minimum(j, n_up - 1), 0)

    def w2_map(c, j, ce, cv, tot):
        return (ce[c], 0, jnp.maximum(j - n_up, 0))

    def o_map(c, j, ce, cv, tot):
        return (c, jnp.maximum(j - n_up, 0))

    grid_spec = pltpu.PrefetchScalarGridSpec(
        num_scalar_prefetch=3,
        grid=(n_chunks, n_up + n_down),
        in_specs=[pl.BlockSpec((ROW_CHUNK, UP_K_TILE // 2), x_map),
                  pl.BlockSpec((None, UP_K_TILE, fdim), w13_map),
                  pl.BlockSpec((None, UP_K_TILE, fdim), w13_map),
                  pl.BlockSpec((None, fdim, DOWN_N_TILE), w2_map)],
        out_specs=pl.BlockSpec((ROW_CHUNK, DOWN_N_TILE // 2), o_map),
        scratch_shapes=[pltpu.VMEM((UP_K_TILE, 2 * fdim), BF16),
                        pltpu.VMEM((ROW_CHUNK, 2 * fdim), F32),
                        pltpu.VMEM((ROW_CHUNK, fdim), BF16),
                        pltpu.VMEM((fdim, DOWN_N_TILE), BF16)],
    )
    return pl.pallas_call(
        functools.partial(_experts_kernel, n_up=n_up, fdim=fdim),
        grid_spec=grid_spec,
        out_shape=jax.ShapeDtypeStruct((n_chunks * ROW_CHUNK, d // 2), I32),
        compiler_params=_cparams(("arbitrary", "arbitrary")),
        name="routed_experts",
    )(chunk_expert, chunk_valid, total, xs, w1, w3, w2)


def _shared_kernel(x_ref, w13_ref, w2_ref, o_ref, *, tf):
    f = pl.program_id(1)

    @pl.when(f == 0)
    def _():
        o_ref[...] = jnp.zeros_like(o_ref)

    rows_per = x_ref.shape[0] // ROW_GROUPS
    for g in range(ROW_GROUPS):
        rows = slice(g * rows_per, (g + 1) * rows_per)
        r = jnp.dot(x_ref[rows, :], w13_ref[...], preferred_element_type=F32)
        h1 = r[:, :tf]
        h = (h1 * jax.nn.sigmoid(h1) * r[:, tf:]).astype(BF16)
        o_ref[rows, :] += jnp.dot(h, w2_ref[...], preferred_element_type=F32)


def _shared_expert(x1b, w13s, w2s, *, bm=1024, tf=SHARED_F_TILE):
    n, d = x1b.shape
    nf = w2s.shape[0] // tf
    return pl.pallas_call(
        functools.partial(_shared_kernel, tf=tf),
        grid=(n // bm, nf),
        in_specs=[pl.BlockSpec((bm, d), lambda i, f: (i, 0)),
                  pl.BlockSpec((d, 2 * tf), lambda i, f: (0, f)),
                  pl.BlockSpec((tf, d), lambda i, f: (f, 0))],
        out_specs=pl.BlockSpec((bm, d), lambda i, f: (i, 0)),
        out_shape=jax.ShapeDtypeStruct((n, d), F32),
        compiler_params=_cparams(("parallel", "arbitrary")),
        name="shared_expert",
    )(x1b, w13s, w2s)


def _combine_kernel(dcur_ref, dnext_ref, x1_ref, ysh_ref, gate_ref, g_ref, b_ref, ys_hbm, o_ref,
                    buf_ref, sem, *, bt):
    i = pl.program_id(0)
    slot = lax.rem(i, 2)

    def issue(dref, s):
        def start(g, c):
            for u in range(ROW_UNROLL):
                for k in range(TOP_K):
                    src = dref[0, 0, (g * ROW_UNROLL + u) * TOP_K + k]
                    pltpu.make_async_copy(ys_hbm.at[pl.ds(src, 1), :],
                                          buf_ref.at[s, k, g, pl.ds(u, 1), :],
                                          sem.at[s]).start(priority=k % 2)
            return c

        lax.fori_loop(0, bt // ROW_UNROLL, start, 0)

    @pl.when(i == 0)
    def _():
        issue(dcur_ref, slot)

    @pl.when(i + 1 < pl.num_programs(0))
    def _():
        issue(dnext_ref, 1 - slot)

    acc = ALPHA * x1_ref[...] + ysh_ref[...]

    for k in range(TOP_K):
        pltpu.make_async_copy(buf_ref.at[slot, k], buf_ref.at[slot, k], sem.at[slot]).wait()
    gate = gate_ref[...]
    half = PACK_TILE // 2
    for k in range(TOP_K):
        yk = jnp.concatenate(
            [_unpack_bf16_pairs(buf_ref[slot, k, :, :, t * half:(t + 1) * half].reshape(bt, half))
             for t in range(acc.shape[1] // PACK_TILE)], axis=1)
        acc = acc + gate[:, k:k + 1] * yk
    o_ref[...] = _layer_norm(acc, g_ref[...], b_ref[...])


def _combine(dest3, x1, ysh, gate, g2, b2, ys, *, bt=TOK_TILE):
    n, d = x1.shape
    nt = n // bt
    row = lambda i: (i, 0)
    fix = lambda i: (0, 0)
    return pl.pallas_call(
        functools.partial(_combine_kernel, bt=bt),
        grid=(nt,),
        in_specs=[pl.BlockSpec((1, 1, bt * TOP_K), lambda i: (i, 0, 0), memory_space=pltpu.SMEM),
                  pl.BlockSpec((1, 1, bt * TOP_K), lambda i: (jnp.minimum(i + 1, nt - 1), 0, 0),
                               memory_space=pltpu.SMEM),
                  pl.BlockSpec((bt, d), row), pl.BlockSpec((bt, d), row),
                  pl.BlockSpec((bt, LANES), row),
                  pl.BlockSpec((1, d), fix), pl.BlockSpec((1, d), fix),
                  pl.BlockSpec(memory_space=pl.ANY)],
        out_specs=pl.BlockSpec((bt, d), row),
        out_shape=jax.ShapeDtypeStruct((n, d), F32),
        scratch_shapes=[pltpu.VMEM((2, TOP_K, bt // ROW_UNROLL, ROW_UNROLL, d // 2), I32),
                        pltpu.SemaphoreType.DMA((2,))],
        compiler_params=_cparams(("arbitrary",)),
        name="combine_ln2",
    )(dest3, dest3, x1, ysh, gate, g2, b2, ys)


def _layer(x, w_in, b_forget, pool_w, pool_scale, w_branch_pool, w_branch_attn, w_out,
           ln1_g, ln1_b, w_router, router_bias, w1, w3, w2, w_shared1, w_shared3, w_shared2,
           ln2_g, ln2_b):
    batch, seq, d = x.shape
    n = batch * seq
    pool_width = w_branch_pool.shape[0]
    attn_width = w_branch_attn.shape[0]
    n_heads = attn_width // HEAD_DIM
    n_experts = w1.shape[0]
    off_q = pool_width
    off_k = off_q + attn_width
    off_v = off_k + attn_width
    off_f = off_v + attn_width
    off_gate = off_f + n_heads

    x2 = x.reshape(n, d)

    w_main = w_in[:, :off_f].astype(BF16)
    wf_pad = jnp.pad(w_in[:, off_f:off_gate], ((0, 0), (0, LANES - n_heads))).astype(BF16)
    bf_pad = jnp.pad(b_forget.astype(F32), (0, LANES - n_heads)).reshape(1, LANES)
    w_gate = w_in[:, off_gate:].astype(BF16)
    piece = jnp.arange(3)[:, None, None]
    src = jnp.arange(LANES)[None, :, None]
    dst = jnp.arange(n_heads * LANES)[None, None, :]
    sel = ((src < n_heads) & (dst == src * LANES + piece)).astype(BF16)
    wr = jnp.pad(w_router.astype(F32), ((0, 0), (0, LANES - n_experts)))
    wr_hi = wr.astype(BF16)
    wr_lo = (wr - wr_hi.astype(F32)).astype(BF16)
    rb_pad = jnp.pad(router_bias.astype(F32), (0, LANES - n_experts)).reshape(1, LANES)
    tf = SHARED_F_TILE
    fs = w_shared1.shape[1]
    w13s = jnp.concatenate([w_shared1.reshape(d, fs // tf, tf), w_shared3.reshape(d, fs // tf, tf)],
                           axis=2).reshape(d, 2 * fs).astype(BF16)

    z = _inproj(x2, w_main, q_lo=off_q, q_hi=off_k, q_scale=HEAD_DIM ** -0.5)
    ka, edge = _forget_bias(x2, wf_pad, bf_pad, sel, batch=batch, seq=seq, n_heads=n_heads)
    n_kblk = seq // ATTN_BLOCK
    edge = edge.reshape(batch, n_kblk, 8, LANES)[:, :, :1, :n_heads]
    blast = edge[:, :, 0, :].transpose(0, 2, 1).reshape(-1)
    pool_o = _pool(z, pool_w.astype(BF16), pool_scale.astype(F32).reshape(1, pool_width),
                   batch=batch, seq=seq, width=pool_width)
    attn_o = _attention(blast, z, ka, batch=batch, seq=seq, n_heads=n_heads,
                        q_col=off_q // HEAD_DIM, k_col=off_k // HEAD_DIM, v_col=off_v // HEAD_DIM)
    merged = _merge(x2, pool_o, attn_o, w_gate, w_branch_pool.astype(BF16),
                    w_branch_attn.astype(BF16))
    x1, x1b, x1p, logits = _out_ln_router(x2, merged, w_out.astype(BF16),
                                     ln1_g.astype(F32).reshape(1, d), ln1_b.astype(F32).reshape(1, d),
                                     wr_hi, wr_lo)

    idx, gate, rank, cnt = _route(logits, rb_pad, n_experts=n_experts)
    counts = cnt[0, :n_experts]
    n_assign = n * TOP_K
    n_chunks = -(-n_assign // ROW_CHUNK) + n_experts
    chunks_per = (counts + ROW_CHUNK - 1) // ROW_CHUNK
    chunk_end = jnp.cumsum(chunks_per)
    chunk_begin = chunk_end - chunks_per
    total = chunk_end[-1]
    group_start = chunk_begin * ROW_CHUNK
    cids = jnp.arange(n_chunks, dtype=I32)
    c_eff = jnp.minimum(cids, total - 1)
    chunk_expert = jnp.minimum(jnp.sum((chunk_end[None, :] <= c_eff[:, None]).astype(I32), axis=1),
                               n_experts - 1).astype(I32)
    chunk_valid = jnp.where(cids < total,
                            jnp.minimum(counts[chunk_expert] - (cids - chunk_begin[chunk_expert]) * ROW_CHUNK,
                                        ROW_CHUNK), 0).astype(I32)
    idx_k = idx[:, :TOP_K]
    dest = group_start[idx_k].astype(I32) + rank[:, :TOP_K]
    dest3 = dest.reshape(n // TOK_TILE, 1, TOK_TILE * TOP_K)
    zstart = ((group_start + counts) // SUB_ROWS * SUB_ROWS).astype(I32)
    rows_total = (n_chunks + 1) * ROW_CHUNK

    xs = _dispatch(zstart, dest3, x1p, rows_total=rows_total, n_experts=n_experts)
    ys = _experts(chunk_expert, chunk_valid, total.reshape(1).astype(I32), xs, w1, w3, w2,
                  n_chunks=n_chunks)
    ysh = _shared_expert(x1b, w13s, w_shared2.astype(BF16))
    out = _combine(dest3, x1, ysh, gate, ln2_g.astype(F32).reshape(1, d),
                   ln2_b.astype(F32).reshape(1, d), ys)
    return out.reshape(batch, seq, d)


def kernel(x, w_in, b_forget, pool_w, pool_scale, w_branch_pool, w_branch_attn, w_out, ln1_g, ln1_b,
           w_router, router_bias, w1, w3, w2, w_shared1, w_shared3, w_shared2, ln2_g, ln2_b):
    for l in range(w_in.shape[0]):
        x = _layer(x, w_in[l], b_forget[l], pool_w[l], pool_scale[l], w_branch_pool[l],
                   w_branch_attn[l], w_out[l], ln1_g[l], ln1_b[l], w_router[l], router_bias[l],
                   w1[l], w3[l], w2[l], w_shared1[l], w_shared3[l], w_shared2[l], ln2_g[l], ln2_b[l])
    return x
```

```python
import functools

import jax
import jax.numpy as jnp
from jax import lax
from jax.experimental import pallas as pl
from jax.experimental.pallas import tpu as pltpu

F32 = jnp.float32
BF16 = jnp.bfloat16
I32 = jnp.int32

POOL_WINDOWS = (2, 4, 8, 16)
HEAD_DIM = 128
TOP_K = 6
ROUTE_SCALE = 1.0
DEPTH = 1
ALPHA = (2 * DEPTH) ** 0.25
LN_EPS = 1e-5

LANES = 128
MAX_WINDOW = max(POOL_WINDOWS)
ATTN_BLOCK = 512
Q_HALVES = 2
SKIP_MARGIN = 110.0
ROW_CHUNK = 1024
SUB_ROWS = 256
PACK_TILE = 512
UP_K_TILE = PACK_TILE
DOWN_N_TILE = 2 * PACK_TILE
SHARED_F_TILE = 256
TOK_TILE = 256
ROW_UNROLL = 8
ROW_GROUPS = 2
VMEM_LIMIT = 56 * 1024 * 1024


def _cparams(sem, vmem=VMEM_LIMIT):
    return pltpu.CompilerParams(dimension_semantics=sem, vmem_limit_bytes=vmem)


def _split3(v):
    hi = v.astype(BF16)
    r1 = v - hi.astype(F32)
    mid = r1.astype(BF16)
    lo = (r1 - mid.astype(F32)).astype(BF16)
    return hi, mid, lo


def _pack_bf16_pairs(v):
    c = v.shape[1] // 2
    lo = lax.bitcast_convert_type(v[:, :c].astype(BF16).astype(F32), I32)
    hi = lax.bitcast_convert_type(v[:, c:].astype(BF16).astype(F32), I32)
    return ((lo >> 16) & 0xFFFF) | (hi & -65536)


def _unpack_bf16_pairs(w):
    lo = lax.bitcast_convert_type(w << 16, F32)
    hi = lax.bitcast_convert_type(w & -65536, F32)
    return jnp.concatenate([lo, hi], axis=1)


def _pack_tiles(v):
    return jnp.concatenate([_pack_bf16_pairs(v[:, g * PACK_TILE:(g + 1) * PACK_TILE])
                            for g in range(v.shape[1] // PACK_TILE)], axis=1)


def _layer_norm(v, g, b):
    mu = jnp.mean(v, axis=-1, keepdims=True)
    d = v - mu
    var = jnp.mean(d * d, axis=-1, keepdims=True)
    return d * lax.rsqrt(var + LN_EPS) * g + b


def _inproj_kernel(x_ref, w_ref, o_ref, xb_ref, *, bn, q_lo, q_hi, q_scale):
    j = pl.program_id(1)

    @pl.when(j == 0)
    def _():
        xb_ref[...] = x_ref[...].astype(BF16)

    acc = jnp.dot(xb_ref[...], w_ref[...], preferred_element_type=F32)
    col0 = j * bn
    s = jnp.where((col0 >= q_lo) & (col0 < q_hi), q_scale, 1.0).astype(F32)
    o_ref[...] = (acc * s).astype(BF16)


def _inproj(x2, w_main, *, n_cols, q_lo, q_hi, q_scale, bm=1024, bn=1024):
    n, d = x2.shape
    c = n_cols
    return pl.pallas_call(
        functools.partial(_inproj_kernel, bn=bn, q_lo=q_lo, q_hi=q_hi, q_scale=q_scale),
        grid=(n // bm, c // bn),
        in_specs=[pl.BlockSpec((bm, d), lambda i, j: (i, 0)),
                  pl.BlockSpec((d, bn), lambda i, j: (0, j))],
        out_specs=pl.BlockSpec((bm, bn), lambda i, j: (i, j)),
        out_shape=jax.ShapeDtypeStruct((n, c), BF16),
        scratch_shapes=[pltpu.VMEM((bm, d), BF16)],
        compiler_params=_cparams(("parallel", "arbitrary")),
        name="inproj",
    )(x2, w_main)


def _forget_kernel(x_ref, wf_ref, b_ref, sel_ref, ka_ref, edge_ref, carry_ref, *, ts, n_heads):
    i = pl.program_id(1)

    @pl.when(i == 0)
    def _():
        carry_ref[...] = jnp.zeros_like(carry_ref)

    z = jnp.dot(x_ref[...].astype(BF16), wf_ref[...], preferred_element_type=F32) + b_ref[...]
    lf = jnp.minimum(z, 0.0) - jnp.log1p(jnp.exp(-jnp.abs(z)))
    lane = lax.broadcasted_iota(I32, lf.shape, 1)
    lf = jnp.where(lane < n_heads, lf, 0.0)

    row = lax.broadcasted_iota(I32, (ts, ts), 0)
    col = lax.broadcasted_iota(I32, (ts, ts), 1)
    tri = jnp.where(col <= row, 1.0, 0.0).astype(BF16)
    hi, mid, lo = _split3(lf)
    cs = (jnp.dot(tri, hi, preferred_element_type=F32)
          + jnp.dot(tri, mid, preferred_element_type=F32)
          + jnp.dot(tri, lo, preferred_element_type=F32)) + carry_ref[...]
    carry_ref[...] = cs[ts - 1:ts, :]
    edge_ref[0] = jnp.broadcast_to(-cs[ts - 1:ts, :], (8, LANES))

    h2, m2, l2 = _split3(-cs)
    ka = (jnp.dot(h2, sel_ref[0], preferred_element_type=F32)
          + jnp.dot(m2, sel_ref[1], preferred_element_type=F32)
          + jnp.dot(l2, sel_ref[2], preferred_element_type=F32))
    ka_ref[...] = ka.astype(BF16)


def _forget_bias(x2, wf_pad, b_pad, sel, *, batch, seq, n_heads, ts=ATTN_BLOCK):
    n, d = x2.shape
    nt = seq // ts
    return pl.pallas_call(
        functools.partial(_forget_kernel, ts=ts, n_heads=n_heads),
        grid=(batch, nt),
        in_specs=[pl.BlockSpec((ts, d), lambda b, i: (b * nt + i, 0)),
                  pl.BlockSpec((d, LANES), lambda b, i: (0, 0)),
                  pl.BlockSpec((1, LANES), lambda b, i: (0, 0)),
                  pl.BlockSpec((3, LANES, n_heads * LANES), lambda b, i: (0, 0, 0))],
        out_specs=[pl.BlockSpec((ts, n_heads * LANES), lambda b, i: (b * nt + i, 0)),
                   pl.BlockSpec((1, 8, LANES), lambda b, i: (b * nt + i, 0, 0))],
        out_shape=[jax.ShapeDtypeStruct((n, n_heads * LANES), BF16),
                   jax.ShapeDtypeStruct((batch * nt, 8, LANES), F32)],
        scratch_shapes=[pltpu.VMEM((1, LANES), F32)],
        compiler_params=_cparams(("arbitrary", "arbitrary")),
        name="forget_bias",
    )(x2, wf_pad, b_pad, sel)


def _pool_kernel(u_ref, halo_ref, pw_ref, sc_ref, o_ref, buf_ref, *, ts, gdim):
    i = pl.program_id(1)
    h = MAX_WINDOW
    halo = halo_ref[...].astype(F32)
    buf_ref[0:h, :] = jnp.where(i > 0, halo, 0.0)
    buf_ref[h:h + ts, :] = u_ref[...].astype(F32)

    t = i * ts + lax.broadcasted_iota(I32, (ts, 1), 0)
    for g, w in enumerate(POOL_WINDOWS):
        cs = slice(g * gdim, (g + 1) * gdim)
        tok = buf_ref[h:h + ts, cs]
        wsum = tok
        for k in range(1, w):
            wsum = wsum + buf_ref[h - k:h - k + ts, cs]
        cnt = jnp.minimum(t + 1, w).astype(F32)
        p = (wsum / cnt - tok).astype(BF16)
        y = jnp.dot(p, pw_ref[g], preferred_element_type=F32) * sc_ref[:, cs]
        o_ref[:, cs] = y.astype(BF16)


def _pool(z, pool_w_b, pool_scale2, *, batch, seq, width, ts=512):
    n = z.shape[0]
    groups = len(POOL_WINDOWS)
    gdim = width // groups
    nt = seq // ts
    hb = ts // MAX_WINDOW
    return pl.pallas_call(
        functools.partial(_pool_kernel, ts=ts, gdim=gdim),
        grid=(batch, nt),
        in_specs=[pl.BlockSpec((ts, width), lambda b, i: (b * nt + i, 0)),
                  pl.BlockSpec((MAX_WINDOW, width),
                               lambda b, i: (jnp.maximum((b * nt + i) * hb - 1, 0), 0)),
                  pl.BlockSpec((groups, gdim, gdim), lambda b, i: (0, 0, 0)),
                  pl.BlockSpec((1, width), lambda b, i: (0, 0))],
        out_specs=pl.BlockSpec((ts, width), lambda b, i: (b * nt + i, 0)),
        out_shape=jax.ShapeDtypeStruct((n, width), BF16),
        scratch_shapes=[pltpu.VMEM((ts + MAX_WINDOW, width), F32)],
        compiler_params=_cparams(("parallel", "parallel")),
        name="pool_mixer",
    )(z, z, pool_w_b, pool_scale2)


def _attn_kernel(blast_ref, q_ref, k_ref, ka_ref, v_ref, o_ref,
                 qa_ref, m_ref, l_ref, acc_ref, kmax_ref, *, blk, n_kblk):
    qi = pl.program_id(2)
    base = (pl.program_id(0) * pl.num_programs(1) + pl.program_id(1)) * n_kblk

    @pl.when(qi == 0)
    def _():
        def knorm(c, run):
            kk = k_ref[pl.ds(pl.multiple_of(c * blk, blk), blk), :].astype(F32)
            return jnp.maximum(run, jnp.sum(kk * kk, axis=1, keepdims=True))

        run = lax.fori_loop(0, n_kblk, knorm, jnp.zeros((blk, 1), F32))
        kmax_ref[0] = jnp.max(jnp.sqrt(run))

    lane = lax.broadcasted_iota(I32, (blk, HEAD_DIM), 1)
    ones3 = jnp.where(lane < 3, 1.0, 0.0).astype(BF16)
    for h in range(Q_HALVES):
        qa_ref[h] = jnp.concatenate([q_ref[h * blk:(h + 1) * blk, :], ones3], axis=1)
    m_ref[...] = jnp.full_like(m_ref, -jnp.inf)
    l_ref[...] = jnp.zeros_like(l_ref)
    acc_ref[...] = jnp.zeros_like(acc_ref)
    reps = blk // LANES

    def step(h, ki, masked):
        ks = pl.multiple_of(ki * blk, blk)
        k_aug = jnp.concatenate([k_ref[pl.ds(ks, blk), :], ka_ref[pl.ds(ks, blk), :]], axis=1)
        s = lax.dot_general(qa_ref[h], k_aug, (((1,), (1,)), ((), ())),
                            preferred_element_type=F32)
        if masked:
            r = lax.broadcasted_iota(I32, (blk, blk), 0)
            c = lax.broadcasted_iota(I32, (blk, blk), 1)
            s = jnp.where(c <= r, s, -jnp.inf)
        m_prev = m_ref[h]
        m_next = jnp.maximum(m_prev, jnp.max(s, axis=1, keepdims=True))
        alpha = jnp.exp(m_prev - m_next)
        p = jnp.exp(s - jnp.concatenate([m_next] * reps, axis=1))
        l_ref[h] = alpha * l_ref[h] + jnp.sum(p, axis=1, keepdims=True)
        acc_ref[h] = alpha * acc_ref[h] + jnp.dot(
            p.astype(BF16), v_ref[pl.ds(ks, blk), :], preferred_element_type=F32)
        m_ref[h] = m_next

    def body(kq, carry):
        for d in range(Q_HALVES):
            for h in range(Q_HALVES):
                step(h, Q_HALVES * kq + d, False)
        return carry

    for h in range(Q_HALVES):
        for d in range(h):
            step(h, Q_HALVES * qi + d, False)
        step(h, Q_HALVES * qi + h, True)

    qq = q_ref[...].astype(F32)
    qmax = jnp.max(jnp.sqrt(jnp.sum(qq * qq, axis=1, keepdims=True)))
    thr = jnp.min(m_ref[...]) - SKIP_MARGIN - qmax * kmax_ref[0]

    def count_skippable(j, a):
        return a + jnp.where(blast_ref[base + j] < thr, 1, 0).astype(I32)

    n_skip = lax.fori_loop(0, Q_HALVES * qi, count_skippable, jnp.int32(0))
    lax.fori_loop(n_skip // Q_HALVES, qi, body, 0)
    for h in range(Q_HALVES):
        o_ref[h * blk:(h + 1) * blk, :] = (acc_ref[h] / l_ref[h]).astype(BF16)


def _attention(blast, z, ka, *, batch, seq, n_heads, q_col, k_col, v_col, blk=ATTN_BLOCK):
    n = z.shape[0]
    qb = Q_HALVES * blk
    nq = seq // qb
    grid_spec = pltpu.PrefetchScalarGridSpec(
        num_scalar_prefetch=1,
        grid=(batch, n_heads, nq),
        in_specs=[pl.BlockSpec((qb, HEAD_DIM), lambda b, h, i, bl: (b * nq + i, q_col + h)),
                  pl.BlockSpec((seq, HEAD_DIM), lambda b, h, i, bl: (b, k_col + h)),
                  pl.BlockSpec((seq, HEAD_DIM), lambda b, h, i, bl: (b, h)),
                  pl.BlockSpec((seq, HEAD_DIM), lambda b, h, i, bl: (b, v_col + h))],
        out_specs=pl.BlockSpec((qb, HEAD_DIM), lambda b, h, i, bl: (b * nq + i, h)),
        scratch_shapes=[pltpu.VMEM((Q_HALVES, blk, 2 * HEAD_DIM), BF16),
                        pltpu.VMEM((Q_HALVES, blk, LANES), F32),
                        pltpu.VMEM((Q_HALVES, blk, LANES), F32),
                        pltpu.VMEM((Q_HALVES, blk, HEAD_DIM), F32),
                        pltpu.SMEM((1,), F32)],
    )
    return pl.pallas_call(
        functools.partial(_attn_kernel, blk=blk, n_kblk=seq // blk),
        grid_spec=grid_spec,
        out_shape=jax.ShapeDtypeStruct((n, n_heads * HEAD_DIM), BF16),
        compiler_params=_cparams(("arbitrary", "arbitrary", "arbitrary")),
        name="fox_attention",
    )(blast, z, z, ka, z)


def _merge_kernel(x_ref, p_ref, a_ref, wg0_ref, wg1_ref, wbp_ref, wba_ref, o_ref, xb_ref):
    j = pl.program_id(1)

    @pl.when(j == 0)
    def _():
        xb_ref[...] = x_ref[...].astype(BF16)

    xb = xb_ref[...]
    g0 = jax.nn.sigmoid(jnp.dot(xb, wg0_ref[...], preferred_element_type=F32))
    g1 = jax.nn.sigmoid(jnp.dot(xb, wg1_ref[...], preferred_element_type=F32))
    yp = jnp.dot(p_ref[...], wbp_ref[...], preferred_element_type=F32)
    ya = jnp.dot(a_ref[...], wba_ref[...], preferred_element_type=F32)
    o_ref[...] = (g0 * yp + g1 * ya).astype(BF16)


def _merge(x2, pool_o, attn_o, wg, wbp, wba, *, bm=1024, bn=512):
    n, d = x2.shape
    wp = pool_o.shape[1]
    wa = attn_o.shape[1]
    nj = d // bn
    return pl.pallas_call(
        _merge_kernel,
        grid=(n // bm, nj),
        in_specs=[pl.BlockSpec((bm, d), lambda i, j: (i, 0)),
                  pl.BlockSpec((bm, wp), lambda i, j: (i, 0)),
                  pl.BlockSpec((bm, wa), lambda i, j: (i, 0)),
                  pl.BlockSpec((d, bn), lambda i, j: (0, j)),
                  pl.BlockSpec((d, bn), lambda i, j: (0, nj + j)),
                  pl.BlockSpec((wp, bn), lambda i, j: (0, j)),
                  pl.BlockSpec((wa, bn), lambda i, j: (0, j))],
        out_specs=pl.BlockSpec((bm, bn), lambda i, j: (i, j)),
        out_shape=jax.ShapeDtypeStruct((n, d), BF16),
        scratch_shapes=[pltpu.VMEM((bm, d), BF16)],
        compiler_params=_cparams(("parallel", "arbitrary")),
        name="branch_merge",
    )(x2, pool_o, attn_o, wg, wg, wbp, wba)


def _out_ln_router_kernel(x_ref, m_ref, wo_ref, g_ref, b_ref, wrh_ref, wrl_ref,
                          x1_ref, x1b_ref, x1p_ref, lg_ref):
    bm = x_ref.shape[0]
    rows_per = bm // ROW_GROUPS
    half = PACK_TILE // 2
    for r in range(ROW_GROUPS):
        rows = slice(r * rows_per, (r + 1) * rows_per)
        mix = jnp.dot(m_ref[rows, :], wo_ref[...], preferred_element_type=F32)
        x1 = _layer_norm(ALPHA * x_ref[rows, :] + mix, g_ref[...], b_ref[...])
        x1_ref[rows, :] = x1
        hi = x1.astype(BF16)
        x1b_ref[rows, :] = hi
        for t in range(x1.shape[1] // PACK_TILE):
            x1p_ref[rows, t * half:(t + 1) * half] = _pack_bf16_pairs(
                x1[:, t * PACK_TILE:(t + 1) * PACK_TILE])
        lo = (x1 - hi.astype(F32)).astype(BF16)
        lg_ref[rows, :] = (jnp.dot(hi, wrh_ref[...], preferred_element_type=F32)
                           + jnp.dot(lo, wrh_ref[...], preferred_element_type=F32)
                           + jnp.dot(hi, wrl_ref[...], preferred_element_type=F32))


def _out_ln_router(x2, merged, wo, g1, b1, wr_hi, wr_lo, *, bm=512):
    n, d = x2.shape
    row = lambda i: (i, 0)
    fix = lambda i: (0, 0)
    return pl.pallas_call(
        _out_ln_router_kernel,
        grid=(n // bm,),
        in_specs=[pl.BlockSpec((bm, d), row), pl.BlockSpec((bm, d), row),
                  pl.BlockSpec((d, d), fix), pl.BlockSpec((1, d), fix), pl.BlockSpec((1, d), fix),
                  pl.BlockSpec((d, LANES), fix), pl.BlockSpec((d, LANES), fix)],
        out_specs=[pl.BlockSpec((bm, d), row), pl.BlockSpec((bm, d), row),
                   pl.BlockSpec((bm, d // 2), row), pl.BlockSpec((bm, LANES), row)],
        out_shape=[jax.ShapeDtypeStruct((n, d), F32), jax.ShapeDtypeStruct((n, d), BF16),
                   jax.ShapeDtypeStruct((n, d // 2), I32), jax.ShapeDtypeStruct((n, LANES), F32)],
        compiler_params=_cparams(("parallel",)),
        name="outproj_ln1_router",
    )(x2, merged, wo, g1, b1, wr_hi, wr_lo)


def _route_kernel(lg_ref, bias_ref, idx_ref, gate_ref, rank_ref, cnt_ref, carry_ref,
                  *, bm, n_experts):
    i = pl.program_id(0)

    @pl.when(i == 0)
    def _():
        carry_ref[...] = jnp.zeros_like(carry_ref)

    lane = lax.broadcasted_iota(I32, (bm, LANES), 1)
    lane_f = lane.astype(F32)
    scores = jax.nn.sigmoid(lg_ref[...])
    sel = jnp.where(lane < n_experts, scores + bias_ref[...], -jnp.inf)

    hits, gates = [], []
    gsum = jnp.zeros((bm, 1), F32)
    member = jnp.zeros((bm, LANES), F32)
    idx_out = jnp.zeros((bm, LANES), F32)
    for k in range(TOP_K):
        m = jnp.max(sel, axis=1, keepdims=True)
        ik = jnp.min(jnp.where(sel == m, lane_f, float(LANES)), axis=1, keepdims=True)
        hit = lane_f == ik
        gk = jnp.sum(jnp.where(hit, scores, 0.0), axis=1, keepdims=True)
        sel = jnp.where(hit, -jnp.inf, sel)
        member = jnp.where(hit, 1.0, member)
        idx_out = jnp.where(lane == k, ik, idx_out)
        gsum = gsum + gk
        hits.append(hit)
        gates.append(gk)

    row = lax.broadcasted_iota(I32, (bm, bm), 0)
    col = lax.broadcasted_iota(I32, (bm, bm), 1)
    before = jnp.where(col < row, 1.0, 0.0).astype(BF16)
    prefix = jnp.dot(before, member.astype(BF16), preferred_element_type=F32) + carry_ref[...]

    gate_out = jnp.zeros((bm, LANES), F32)
    rank_out = jnp.zeros((bm, LANES), F32)
    for k in range(TOP_K):
        rk = jnp.sum(jnp.where(hits[k], prefix, 0.0), axis=1, keepdims=True)
        rank_out = jnp.where(lane == k, rk, rank_out)
        gate_out = jnp.where(lane == k, gates[k] / gsum * ROUTE_SCALE, gate_out)

    total = carry_ref[...] + jnp.sum(member, axis=0, keepdims=True)
    carry_ref[...] = total
    idx_ref[...] = idx_out.astype(I32)
    gate_ref[...] = gate_out
    rank_ref[...] = rank_out.astype(I32)
    cnt_ref[...] = jnp.broadcast_to(total, cnt_ref.shape).astype(I32)


def _route(logits, bias_pad, *, n_experts, bm=2 * TOK_TILE):
    n = logits.shape[0]
    row = lambda i: (i, 0)
    fix = lambda i: (0, 0)
    return pl.pallas_call(
        functools.partial(_route_kernel, bm=bm, n_experts=n_experts),
        grid=(n // bm,),
        in_specs=[pl.BlockSpec((bm, LANES), row), pl.BlockSpec((1, LANES), fix)],
        out_specs=[pl.BlockSpec((bm, LANES), row), pl.BlockSpec((bm, LANES), row),
                   pl.BlockSpec((bm, LANES), row), pl.BlockSpec((8, LANES), fix)],
        out_shape=[jax.ShapeDtypeStruct((n, LANES), I32), jax.ShapeDtypeStruct((n, LANES), F32),
                   jax.ShapeDtypeStruct((n, LANES), I32), jax.ShapeDtypeStruct((8, LANES), I32)],
        scratch_shapes=[pltpu.VMEM((1, LANES), F32)],
        compiler_params=_cparams(("arbitrary",)),
        name="route_topk",
    )(logits, bias_pad)


def _dispatch_kernel(zs_ref, dest_ref, x_ref, xs_hbm, zero_ref, sem, zsem, *, bt, n_experts):
    i = pl.program_id(0)

    def zero_copy(e):
        zs = pl.multiple_of(zs_ref[e], SUB_ROWS)
        return pltpu.make_async_copy(zero_ref, xs_hbm.at[pl.ds(zs, SUB_ROWS), :], zsem)

    @pl.when(i == 0)
    def _():
        zero_ref[...] = jnp.zeros_like(zero_ref)

        def zstart(e, c):
            zero_copy(e).start()
            return c

        def zwait(e, c):
            zero_copy(e).wait()
            return c

        lax.fori_loop(0, n_experts, zstart, 0)
        lax.fori_loop(0, n_experts, zwait, 0)

    def start(g, c):
        for u in range(ROW_UNROLL):
            for k in range(TOP_K):
                dst = dest_ref[0, 0, (g * ROW_UNROLL + u) * TOP_K + k]
                pltpu.make_async_copy(x_ref.at[g, pl.ds(u, 1), :], xs_hbm.at[pl.ds(dst, 1), :],
                                      sem).start(priority=k % 2)
        return c

    lax.fori_loop(0, bt // ROW_UNROLL, start, 0)
    for k in range(TOP_K):
        pltpu.make_async_copy(x_ref, x_ref, sem).wait()


def _dispatch(zstart, dest3, x1p, *, rows_total, n_experts, bt=TOK_TILE):
    n, d = x1p.shape
    grid_spec = pltpu.PrefetchScalarGridSpec(
        num_scalar_prefetch=1,
        grid=(n // bt,),
        in_specs=[pl.BlockSpec((1, 1, bt * TOP_K), lambda i, zs: (i, 0, 0),
                               memory_space=pltpu.SMEM),
                  pl.BlockSpec((bt // ROW_UNROLL, ROW_UNROLL, d), lambda i, zs: (i, 0, 0))],
        out_specs=pl.BlockSpec(memory_space=pl.ANY),
        scratch_shapes=[pltpu.VMEM((SUB_ROWS, d), I32),
                        pltpu.SemaphoreType.DMA(()),
                        pltpu.SemaphoreType.DMA(())],
    )
    return pl.pallas_call(
        functools.partial(_dispatch_kernel, bt=bt, n_experts=n_experts),
        grid_spec=grid_spec,
        out_shape=jax.ShapeDtypeStruct((rows_total, d), I32),
        compiler_params=_cparams(("arbitrary",)),
        name="dispatch_rows",
    )(zstart, dest3, x1p.reshape(n // ROW_UNROLL, ROW_UNROLL, d))


def _experts_kernel(ce_ref, cv_ref, tot_ref, x_ref, w1_ref, w3_ref, w2_ref, o_ref,
                    w13_ref, acc_ref, h_ref, w2b_ref, *, n_up, fdim):
    c = pl.program_id(0)
    j = pl.program_id(1)
    nv = cv_ref[c]
    del ce_ref, tot_ref
    all_rows = slice(0, ROW_CHUNK)
    sub_blocks = [slice(sb * SUB_ROWS, (sb + 1) * SUB_ROWS) for sb in range(ROW_CHUNK // SUB_ROWS)]

    def up_rows(rows, first, last):
        xb = _unpack_bf16_pairs(x_ref[rows, :]).astype(BF16)
        part = jnp.dot(xb, w13_ref[...], preferred_element_type=F32)
        if not first:
            part = part + acc_ref[rows, :]
        if last:
            h1 = part[:, :fdim]
            h_ref[rows, :] = (h1 * jax.nn.sigmoid(h1) * part[:, fdim:]).astype(BF16)
        else:
            acc_ref[rows, :] = part

    def cast_up_weights():
        w13_ref[:, :fdim] = w1_ref[...].astype(BF16)
        w13_ref[:, fdim:] = w3_ref[...].astype(BF16)

    def up(first, last):
        @pl.when(nv > ROW_CHUNK - SUB_ROWS)
        def _():
            cast_up_weights()
            up_rows(all_rows, first, last)

        @pl.when(nv <= ROW_CHUNK - SUB_ROWS)
        def _():
            cast_up_weights()
            for rows in sub_blocks:
                @pl.when(rows.start < nv)
                def _():
                    up_rows(rows, first, last)

    @pl.when(j == 0)
    def _():
        up(True, n_up == 1)

    if n_up > 2:
        @pl.when((j > 0) & (j < n_up - 1))
        def _():
            up(False, False)

    if n_up > 1:
        @pl.when(j == n_up - 1)
        def _():
            up(False, True)

    @pl.when(j >= n_up)
    def _():
        @pl.when(nv > ROW_CHUNK - SUB_ROWS)
        def _():
            w2b_ref[...] = w2_ref[...].astype(BF16)
            o_ref[...] = _pack_tiles(
                jnp.dot(h_ref[...], w2b_ref[...], preferred_element_type=F32))

        @pl.when(nv <= ROW_CHUNK - SUB_ROWS)
        def _():
            w2b_ref[...] = w2_ref[...].astype(BF16)
            for rows in sub_blocks:
                @pl.when(rows.start < nv)
                def _():
                    o_ref[rows, :] = _pack_tiles(
                        jnp.dot(h_ref[rows, :], w2b_ref[...], preferred_element_type=F32))

                @pl.when(rows.start >= nv)
                def _():
                    o_ref[rows, :] = jnp.zeros((SUB_ROWS, o_ref.shape[1]), I32)


def _experts(chunk_expert, chunk_valid, total, xs, w1, w3, w2, *, n_chunks):
    d = w1.shape[1]
    fdim = w1.shape[2]
    n_up = d // UP_K_TILE
    n_down = d // DOWN_N_TILE

    def x_map(c, j, ce, cv, tot):
        return (jnp.minimum(c, tot[0] - 1), jnp.minimum(j, n_up - 1))

    def w13_map(c, j, ce, cv, tot):
        return (ce[c], jnp.minimum(j, n_up - 1), 0)

    def w2_map(c, j, ce, cv, tot):
        return (ce[c], 0, jnp.maximum(j - n_up, 0))

    def o_map(c, j, ce, cv, tot):
        return (c, jnp.maximum(j - n_up, 0))

    grid_spec = pltpu.PrefetchScalarGridSpec(
        num_scalar_prefetch=3,
        grid=(n_chunks, n_up + n_down),
        in_specs=[pl.BlockSpec((ROW_CHUNK, UP_K_TILE // 2), x_map),
                  pl.BlockSpec((None, UP_K_TILE, fdim), w13_map),
                  pl.BlockSpec((None, UP_K_TILE, fdim), w13_map),
                  pl.BlockSpec((None, fdim, DOWN_N_TILE), w2_map)],
        out_specs=pl.BlockSpec((ROW_CHUNK, DOWN_N_TILE // 2), o_map),
        scratch_shapes=[pltpu.VMEM((UP_K_TILE, 2 * fdim), BF16),
                        pltpu.VMEM((ROW_CHUNK, 2 * fdim), F32),
                        pltpu.VMEM((ROW_CHUNK, fdim), BF16),
                        pltpu.VMEM((fdim, DOWN_N_TILE), BF16)],
    )
    return pl.pallas_call(
        functools.partial(_experts_kernel, n_up=n_up, fdim=fdim),
        grid_spec=grid_spec,
        out_shape=jax.ShapeDtypeStruct((n_chunks * ROW_CHUNK, d // 2), I32),
        compiler_params=_cparams(("arbitrary", "arbitrary")),
        name="routed_experts",
    )(chunk_expert, chunk_valid, total, xs, w1, w3, w2)


def _shared_kernel(x_ref, w13_ref, w2_ref, o_ref, *, tf):
    f = pl.program_id(1)

    @pl.when(f == 0)
    def _():
        o_ref[...] = jnp.zeros_like(o_ref)

    rows_per = x_ref.shape[0] // ROW_GROUPS
    for g in range(ROW_GROUPS):
        rows = slice(g * rows_per, (g + 1) * rows_per)
        r = jnp.dot(x_ref[rows, :], w13_ref[...], preferred_element_type=F32)
        h1 = r[:, :tf]
        h = (h1 * jax.nn.sigmoid(h1) * r[:, tf:]).astype(BF16)
        o_ref[rows, :] += jnp.dot(h, w2_ref[...], preferred_element_type=F32)


def _shared_expert(x1b, w13s, w2s, *, bm=1024, tf=SHARED_F_TILE):
    n, d = x1b.shape
    nf = w2s.shape[0] // tf
    return pl.pallas_call(
        functools.partial(_shared_kernel, tf=tf),
        grid=(n // bm, nf),
        in_specs=[pl.BlockSpec((bm, d), lambda i, f: (i, 0)),
                  pl.BlockSpec((d, 2 * tf), lambda i, f: (0, f)),
                  pl.BlockSpec((tf, d), lambda i, f: (f, 0))],
        out_specs=pl.BlockSpec((bm, d), lambda i, f: (i, 0)),
        out_shape=jax.ShapeDtypeStruct((n, d), F32),
        compiler_params=_cparams(("parallel", "arbitrary")),
        name="shared_expert",
    )(x1b, w13s, w2s)


def _combine_kernel(dcur_ref, dnext_ref, x1_ref, ysh_ref, gate_ref, g_ref, b_ref, ys_hbm, o_ref,
                    buf_ref, sem, *, bt):
    i = pl.program_id(0)
    slot = lax.rem(i, 2)

    def issue(dref, s):
        def start(g, c):
            for u in range(ROW_UNROLL):
                for k in range(TOP_K):
                    src = dref[0, 0, (g * ROW_UNROLL + u) * TOP_K + k]
                    pltpu.make_async_copy(ys_hbm.at[pl.ds(src, 1), :],
                                          buf_ref.at[s, k, g, pl.ds(u, 1), :],
                                          sem.at[s]).start(priority=k % 2)
            return c

        lax.fori_loop(0, bt // ROW_UNROLL, start, 0)

    @pl.when(i == 0)
    def _():
        issue(dcur_ref, slot)

    @pl.when(i + 1 < pl.num_programs(0))
    def _():
        issue(dnext_ref, 1 - slot)

    acc = ALPHA * x1_ref[...] + ysh_ref[...]

    for k in range(TOP_K):
        pltpu.make_async_copy(buf_ref.at[slot, k], buf_ref.at[slot, k], sem.at[slot]).wait()
    gate = gate_ref[...]
    half = PACK_TILE // 2
    for k in range(TOP_K):
        yk = jnp.concatenate(
            [_unpack_bf16_pairs(buf_ref[slot, k, :, :, t * half:(t + 1) * half].reshape(bt, half))
             for t in range(acc.shape[1] // PACK_TILE)], axis=1)
        acc = acc + gate[:, k:k + 1] * yk
    o_ref[...] = _layer_norm(acc, g_ref[...], b_ref[...])


def _combine(dest3, x1, ysh, gate, g2, b2, ys, *, bt=TOK_TILE):
    n, d = x1.shape
    nt = n // bt
    row = lambda i: (i, 0)
    fix = lambda i: (0, 0)
    return pl.pallas_call(
        functools.partial(_combine_kernel, bt=bt),
        grid=(nt,),
        in_specs=[pl.BlockSpec((1, 1, bt * TOP_K), lambda i: (i, 0, 0), memory_space=pltpu.SMEM),
                  pl.BlockSpec((1, 1, bt * TOP_K), lambda i: (jnp.minimum(i + 1, nt - 1), 0, 0),
                               memory_space=pltpu.SMEM),
                  pl.BlockSpec((bt, d), row), pl.BlockSpec((bt, d), row),
                  pl.BlockSpec((bt, LANES), row),
                  pl.BlockSpec((1, d), fix), pl.BlockSpec((1, d), fix),
                  pl.BlockSpec(memory_space=pl.ANY)],
        out_specs=pl.BlockSpec((bt, d), row),
        out_shape=jax.ShapeDtypeStruct((n, d), F32),
        scratch_shapes=[pltpu.VMEM((2, TOP_K, bt // ROW_UNROLL, ROW_UNROLL, d // 2), I32),
                        pltpu.SemaphoreType.DMA((2,))],
        compiler_params=_cparams(("arbitrary",)),
        name="combine_ln2",
    )(dest3, dest3, x1, ysh, gate, g2, b2, ys)


def _layer(x, w_in, b_forget, pool_w, pool_scale, w_branch_pool, w_branch_attn, w_out,
           ln1_g, ln1_b, w_router, router_bias, w1, w3, w2, w_shared1, w_shared3, w_shared2,
           ln2_g, ln2_b):
    batch, seq, d = x.shape
    n = batch * seq
    pool_width = w_branch_pool.shape[0]
    attn_width = w_branch_attn.shape[0]
    n_heads = attn_width // HEAD_DIM
    n_experts = w1.shape[0]
    off_q = pool_width
    off_k = off_q + attn_width
    off_v = off_k + attn_width
    off_f = off_v + attn_width
    off_gate = off_f + n_heads

    x2 = x.reshape(n, d)

    w_in_b = w_in.astype(BF16)
    wf_pad = jnp.pad(w_in_b[:, off_f:off_gate], ((0, 0), (0, LANES - n_heads)))
    bf_pad = jnp.pad(b_forget.astype(F32), (0, LANES - n_heads)).reshape(1, LANES)
    w_gate = w_in_b[:, off_gate:]
    piece = jnp.arange(3)[:, None, None]
    src = jnp.arange(LANES)[None, :, None]
    dst = jnp.arange(n_heads * LANES)[None, None, :]
    sel = ((src < n_heads) & (dst == src * LANES + piece)).astype(BF16)
    wr = jnp.pad(w_router.astype(F32), ((0, 0), (0, LANES - n_experts)))
    wr_hi = wr.astype(BF16)
    wr_lo = (wr - wr_hi.astype(F32)).astype(BF16)
    rb_pad = jnp.pad(router_bias.astype(F32), (0, LANES - n_experts)).reshape(1, LANES)
    tf = SHARED_F_TILE
    fs = w_shared1.shape[1]
    w13s = jnp.concatenate([w_shared1.reshape(d, fs // tf, tf), w_shared3.reshape(d, fs // tf, tf)],
                           axis=2).reshape(d, 2 * fs).astype(BF16)

    z = _inproj(x2, w_in_b, n_cols=off_f, q_lo=off_q, q_hi=off_k, q_scale=HEAD_DIM ** -0.5)
    ka, edge = _forget_bias(x2, wf_pad, bf_pad, sel, batch=batch, seq=seq, n_heads=n_heads)
    n_kblk = seq // ATTN_BLOCK
    edge = edge.reshape(batch, n_kblk, 8, LANES)[:, :, :1, :n_heads]
    blast = edge[:, :, 0, :].transpose(0, 2, 1).reshape(-1)
    pool_o = _pool(z, pool_w.astype(BF16), pool_scale.astype(F32).reshape(1, pool_width),
                   batch=batch, seq=seq, width=pool_width)
    attn_o = _attention(blast, z, ka, batch=batch, seq=seq, n_heads=n_heads,
                        q_col=off_q // HEAD_DIM, k_col=off_k // HEAD_DIM, v_col=off_v // HEAD_DIM)
    merged = _merge(x2, pool_o, attn_o, w_gate, w_branch_pool.astype(BF16),
                    w_branch_attn.astype(BF16))
    x1, x1b, x1p, logits = _out_ln_router(x2, merged, w_out.astype(BF16),
                                     ln1_g.astype(F32).reshape(1, d), ln1_b.astype(F32).reshape(1, d),
                                     wr_hi, wr_lo)

    idx, gate, rank, cnt = _route(logits, rb_pad, n_experts=n_experts)
    counts = cnt[0, :n_experts]
    n_assign = n * TOP_K
    n_chunks = -(-n_assign // ROW_CHUNK) + n_experts
    chunks_per = (counts + ROW_CHUNK - 1) // ROW_CHUNK
    chunk_end = jnp.cumsum(chunks_per)
    chunk_begin = chunk_end - chunks_per
    total = chunk_end[-1]
    group_start = chunk_begin * ROW_CHUNK
    cids = jnp.arange(n_chunks, dtype=I32)
    c_eff = jnp.minimum(cids, total - 1)
    chunk_expert = jnp.minimum(jnp.sum((chunk_end[None, :] <= c_eff[:, None]).astype(I32), axis=1),
                               n_experts - 1).astype(I32)
    chunk_valid = jnp.where(cids < total,
                            jnp.minimum(counts[chunk_expert] - (cids - chunk_begin[chunk_expert]) * ROW_CHUNK,
                                        ROW_CHUNK), 0).astype(I32)
    idx_k = idx[:, :TOP_K]
    onehot = (idx_k[:, :, None] == jnp.arange(n_experts, dtype=I32)).astype(BF16)
    digits = jnp.stack([chunk_begin // 256, chunk_begin % 256], axis=1).astype(BF16)
    picked = jnp.dot(onehot.reshape(n * TOP_K, n_experts), digits, preferred_element_type=F32)
    first_chunk = (picked[:, 0] * 256.0 + picked[:, 1]).astype(I32).reshape(n, TOP_K)
    dest = first_chunk * ROW_CHUNK + rank[:, :TOP_K]
    dest3 = dest.reshape(n // TOK_TILE, 1, TOK_TILE * TOP_K)
    zstart = ((group_start + counts) // SUB_ROWS * SUB_ROWS).astype(I32)
    rows_total = (n_chunks + 1) * ROW_CHUNK

    xs = _dispatch(zstart, dest3, x1p, rows_total=rows_total, n_experts=n_experts)
    ys = _experts(chunk_expert, chunk_valid, total.reshape(1).astype(I32), xs, w1, w3, w2,
                  n_chunks=n_chunks)
    ysh = _shared_expert(x1b, w13s, w_shared2.astype(BF16))
    out = _combine(dest3, x1, ysh, gate, ln2_g.astype(F32).reshape(1, d),
                   ln2_b.astype(F32).reshape(1, d), ys)
    return out.reshape(batch, seq, d)


def kernel(x, w_in, b_forget, pool_w, pool_scale, w_branch_pool, w_branch_attn, w_out, ln1_g, ln1_b,
           w_router, router_bias, w1, w3, w2, w_shared1, w_shared3, w_shared2, ln2_g, ln2_b):
    for l in range(w_in.shape[0]):
        x = _layer(x, w_in[l], b_forget[l], pool_w[l], pool_scale[l], w_branch_pool[l],
                   w_branch_attn[l], w_out[l], ln1_g[l], ln1_b[l], w_router[l], router_bias[l],
                   w1[l], w3[l], w2[l], w_shared1[l], w_shared3[l], w_shared2[l], ln2_g[l], ln2_b[l])
    return x
```
